```python
import jax
import jax.numpy as jnp
from jax import lax
import numpy as np

D_MODEL = 1024
BATCH = 8
SEQ = 4096
DEPTH = 4

GRID_W = 64
CTX_LEN = 256
Q_BLOCK = 128
ROPE_THETA = 10000.0
NORM_EPS = 1e-6
N_MOD = 6

MLA_HEADS = 4
MLA_Q_RANK = 384
MLA_KV_RANK = 256
MLA_NOPE = 128
MLA_ROPE = 64
MLA_V = 128
MLA_SCALE = (MLA_NOPE + MLA_ROPE) ** -0.5
GQA_HEADS = 4
GQA_KV_HEADS = 2
GQA_GROUP = GQA_HEADS // GQA_KV_HEADS
GQA_HEAD_DIM = 128
GQA_SCALE = GQA_HEAD_DIM ** -0.5
_S1 = MLA_Q_RANK
_S2 = _S1 + MLA_KV_RANK
_S3 = _S2 + MLA_ROPE
_S4 = _S3 + GQA_HEADS * GQA_HEAD_DIM
_S5 = _S4 + GQA_KV_HEADS * GQA_HEAD_DIM
ATTN_SPLITS = (_S1, _S2, _S3, _S4, _S5)
ATTN_IN = _S5 + GQA_KV_HEADS * GQA_HEAD_DIM
ATTN_OUT = MLA_HEADS * MLA_V + GQA_HEADS * GQA_HEAD_DIM

RWKV_HEAD = 64
RWKV_HEADS = D_MODEL // RWKV_HEAD
DECAY_LORA = 64
ICLR_LORA = 64
VRES_LORA = 32
GATE_LORA = 128
RWKV_GN_EPS = 64e-5
N_MIX = 6

N_EXPERTS = 32
TOP_K = 4
EXPERT_FF = D_MODEL
SWIGLU_ALPHA = 1.702
SWIGLU_LIMIT = 7.0

N_EVEN = (DEPTH + 1) // 2
N_ODD = DEPTH // 2
N_VRES = max(N_ODD - 1, 0)

kernel_name = 'hybrid_mla_gqa_rwkv7_moe_dit_trunk'


def rmsnorm(x, g):
    xf = x.astype(jnp.float32)
    y = xf * lax.rsqrt(jnp.mean(xf * xf, axis=-1, keepdims=True) + NORM_EPS)
    return (y * g.astype(jnp.float32)).astype(x.dtype)


def modulate(h, shift, scale):
    return h * (1 + scale) + shift


def axial_angles(n_tokens, rot_dim):
    rows = n_tokens // GRID_W
    row = jnp.repeat(jnp.arange(rows, dtype=jnp.float32), GRID_W)
    col = jnp.tile(jnp.arange(GRID_W, dtype=jnp.float32), rows)
    half = rot_dim // 2
    inv_freq = ROPE_THETA ** (-jnp.arange(0, half, 2, dtype=jnp.float32) / half)
    return row[:, None] * inv_freq[None, :], col[:, None] * inv_freq[None, :]


def _rotate(x, ang):
    cos = jnp.cos(ang)[None, :, None, :].astype(x.dtype)
    sin = jnp.sin(ang)[None, :, None, :].astype(x.dtype)
    x1, x2 = jnp.split(x, 2, axis=-1)
    return jnp.concatenate([x1 * cos - x2 * sin, x2 * cos + x1 * sin], axis=-1)


def axial_rope(x, angles):
    if angles is None:
        return x
    x_row, x_col = jnp.split(x, 2, axis=-1)
    return jnp.concatenate([_rotate(x_row, angles[0]), _rotate(x_col, angles[1])], axis=-1)


def softmax_attend(q, k, v, scale):
    s = jnp.einsum('bqhgd,bkhd->bhgqk', q, k).astype(jnp.float32) * scale
    p = jax.nn.softmax(s, axis=-1).astype(v.dtype)
    return jnp.einsum('bhgqk,bkhd->bqhgd', p, v)


def latent_attention(q, k, v, k_ctx, v_ctx, scale):
    B, T = q.shape[:2]
    k_all = jnp.concatenate([k_ctx, k], axis=1)
    v_all = jnp.concatenate([v_ctx, v], axis=1)
    qb = jnp.swapaxes(q.reshape(B, T // Q_BLOCK, Q_BLOCK, *q.shape[2:]), 0, 1)
    ob = lax.map(lambda q_blk: softmax_attend(q_blk, k_all, v_all, scale), qb)
    return jnp.swapaxes(ob, 0, 1).reshape(B, T, *ob.shape[3:])


def attn_mixer(h_lat, h_ctx, p, ang_mla, ang_gqa, with_ctx):
    def project(h, a_mla, a_gqa):
        B, T, _ = h.shape
        zq, zkv, zr, gq, gk, gv = jnp.split(h @ p['w_in'], ATTN_SPLITS, axis=-1)
        q = (rmsnorm(zq, p['q_norm']) @ p['w_uq']).reshape(B, T, MLA_HEADS, MLA_NOPE + MLA_ROPE)
        kv = (rmsnorm(zkv, p['kv_norm']) @ p['w_ukv']).reshape(B, T, MLA_HEADS, MLA_NOPE + MLA_V)
        q_nope, q_rope = jnp.split(q, [MLA_NOPE], axis=-1)
        k_nope, v_mla = jnp.split(kv, [MLA_NOPE], axis=-1)
        q_mla = jnp.concatenate([rmsnorm(q_nope, p['qn_g']), axial_rope(rmsnorm(q_rope, p['qr_g']), a_mla)], axis=-1)
        k_rope = axial_rope(rmsnorm(zr, p['kr_g'])[:, :, None, :], a_mla)
        k_mla = jnp.concatenate([rmsnorm(k_nope, p['kn_g']), jnp.broadcast_to(k_rope, (B, T, MLA_HEADS, MLA_ROPE))], axis=-1)
        q_gqa = axial_rope(rmsnorm(gq.reshape(B, T, GQA_HEADS, GQA_HEAD_DIM), p['gq_g']), a_gqa)
        k_gqa = axial_rope(rmsnorm(gk.reshape(B, T, GQA_KV_HEADS, GQA_HEAD_DIM), p['gk_g']), a_gqa)
        v_gqa = gv.reshape(B, T, GQA_KV_HEADS, GQA_HEAD_DIM)
        return ((q_mla[:, :, :, None, :], k_mla, v_mla),
                (q_gqa.reshape(B, T, GQA_KV_HEADS, GQA_GROUP, GQA_HEAD_DIM), k_gqa, v_gqa))

    def merge(o_mla, o_gqa):
        B, T = o_mla.shape[:2]
        return jnp.concatenate([o_mla.reshape(B, T, -1), o_gqa.reshape(B, T, -1)], axis=-1) @ p['w_out']

    (mq, mk, mv), (gq, gk, gv) = project(h_lat, ang_mla, ang_gqa)
    (cmq, cmk, cmv), (cgq, cgk, cgv) = project(h_ctx, None, None)
    o_lat = merge(latent_attention(mq, mk, mv, cmk, cmv, MLA_SCALE),
                  latent_attention(gq, gk, gv, cgk, cgv, GQA_SCALE))
    o_ctx = None
    if with_ctx:
        o_ctx = merge(softmax_attend(cmq, cmk, cmv, MLA_SCALE), softmax_attend(cgq, cgk, cgv, GQA_SCALE))
    return o_lat, o_ctx


def centred_shift(h):
    prev = jnp.pad(h[:, :-1], ((0, 0), (1, 0), (0, 0)))
    nxt = jnp.pad(h[:, 1:], ((0, 0), (0, 1), (0, 0)))
    return 0.5 * (prev + nxt)


def rwkv_features(h, p, v_first):
    B, T, _ = h.shape
    heads = lambda t: t.reshape(B, T, RWKV_HEADS, RWKV_HEAD)
    xx = centred_shift(h) - h
    xr, xw, xk, xv, xa, xg = (h + xx * p['mix'][m] for m in range(N_MIX))
    k = xk @ p['w_k']
    v = xv @ p['w_v']
    v_raw = v
    if v_first is not None:
        v = v + (v_first - v) * jax.nn.sigmoid(p['v0'] + (xv @ p['v1']) @ p['v2'])
    kk = heads(k * p['k_k']).astype(jnp.float32)
    kk = kk * lax.rsqrt(jnp.maximum(jnp.sum(kk * kk, axis=-1, keepdims=True), 1e-24))
    decay, k_dir, b_dir = [], [], []
    for d in range(2):
        z = (p['w0'][d] + jnp.tanh(xw @ p['w1'][d]) @ p['w2'][d]).astype(jnp.float32)
        decay.append(heads(jnp.exp(-jnp.exp(-jax.nn.softplus(-z) - 0.5))))
        a = jax.nn.sigmoid(p['a0'][d] + (xa @ p['a1'][d]) @ p['a2'][d])
        k_dir.append(heads(k * (1 + (a - 1) * p['k_a'])))
        b_dir.append(kk * heads(a).astype(jnp.float32))
    return dict(r=heads(xr @ p['w_r']), v=heads(v), kk=kk, xg=xg, decay=decay, k=k_dir, b=b_dir, v_raw=v_raw)


def wkv7_scan(state0, f, d, reverse):
    xs = tuple(jnp.swapaxes(t.astype(jnp.float32), 0, 1)
               for t in (f['r'], f['decay'][d], f['k'][d], f['v'], f['kk'], f['b'][d]))

    def step(S, inp):
        r, w, k, v, kk, b = inp
        sa = jnp.einsum('bhvk,bhk->bhv', S, -kk)
        S = S * w[:, :, None, :] + sa[..., None] * b[:, :, None, :] + v[..., None] * k[:, :, None, :]
        return S, jnp.einsum('bhvk,bhk->bhv', S, r)

    state, y = lax.scan(step, state0, xs, reverse=reverse)
    return state, jnp.swapaxes(y, 0, 1)


def rwkv_output(y, f, p):
    B, T, H, N = y.shape
    mu = jnp.mean(y, axis=-1, keepdims=True)
    var = jnp.mean(jnp.square(y - mu), axis=-1, keepdims=True)
    yn = (y - mu) * lax.rsqrt(var + RWKV_GN_EPS)
    yn = (yn * p['ln_w'].reshape(H, N) + p['ln_b'].reshape(H, N)).astype(f['v'].dtype)
    bonus = jnp.sum(f['r'] * (f['k'][0] + f['k'][1]) * p['r_k'], axis=-1, keepdims=True) * f['v']
    g = jax.nn.sigmoid(f['xg'] @ p['g1']) @ p['g2']
    return ((yn + bonus).reshape(B, T, H * N) * g) @ p['w_o']


def rwkv_mixer(h_lat, h_ctx, p, vf_lat, vf_ctx, with_ctx):
    f_lat = rwkv_features(h_lat, p, vf_lat)
    f_ctx = rwkv_features(h_ctx, p, vf_ctx)
    state0 = jnp.zeros((h_lat.shape[0], RWKV_HEADS, RWKV_HEAD, RWKV_HEAD), jnp.float32)
    y_lat, y_ctx = 0.0, 0.0
    for d, rev in enumerate((False, True)):
        s_ctx, yc = wkv7_scan(state0, f_ctx, d, rev)
        _, yl = wkv7_scan(s_ctx, f_lat, d, rev)
        y_lat = y_lat + yl
        y_ctx = y_ctx + yc
    o_lat = rwkv_output(y_lat, f_lat, p)
    o_ctx = rwkv_output(y_ctx, f_ctx, p) if with_ctx else None
    return o_lat, o_ctx, f_lat['v_raw'], f_ctx['v_raw']


def clamped_swiglu(u):
    u_glu, u_lin = u[..., ::2], u[..., 1::2]
    u_glu = jnp.minimum(u_glu, SWIGLU_LIMIT)
    u_lin = jnp.clip(u_lin, -SWIGLU_LIMIT, SWIGLU_LIMIT)
    return u_glu * jax.nn.sigmoid(SWIGLU_ALPHA * u_glu) * (u_lin + 1)


def moe(h, router_w, router_b, w1, b1, w2, b2):
    logits = (h @ router_w + router_b).astype(jnp.float32)
    top_val, top_idx = lax.top_k(logits, TOP_K)
    top_p = jax.nn.softmax(top_val, axis=-1)
    gates = jnp.einsum('nk,nke->ne', top_p, jax.nn.one_hot(top_idx, N_EXPERTS, dtype=jnp.float32)).astype(h.dtype)
    out = jnp.zeros_like(h)
    for e in range(N_EXPERTS):
        y = clamped_swiglu(h @ w1[e] + b1[e]) @ w2[e] + b2[e]
        out = out + gates[:, e:e + 1] * y
    return out


def setup_inputs(seed: int = 0) -> dict:
    key = jax.random.key(seed)
    ks = iter(jax.random.split(key, 64))

    def nrm(shape, scale):
        return jax.random.normal(next(ks), shape, jnp.float32) * scale

    def gain(shape):
        return 1.0 + nrm(shape, 0.02)

    def unif(shape, lo, hi):
        return jax.random.uniform(next(ks), shape, jnp.float32, lo, hi)

    D, E, F = D_MODEL, N_EXPERTS, EXPERT_FF
    H, N = RWKV_HEADS, RWKV_HEAD
    return {
        'x': nrm((BATCH, SEQ, D), 1.0),
        'c': nrm((BATCH, D), 1.0),
        'ctx': nrm((BATCH, CTX_LEN, D), 1.0),
        'c_ctx': nrm((D,), 1.0),
        'ada_w': nrm((DEPTH, D, N_MOD * D), 0.5 * D ** -0.5),
        'ada_b': nrm((DEPTH, N_MOD * D), 0.02),
        'norm_mix': gain((DEPTH, D)),
        'norm_ffn': gain((DEPTH, D)),
        'attn_w_in': nrm((N_EVEN, D, ATTN_IN), D ** -0.5),
        'mla_q_norm': gain((N_EVEN, MLA_Q_RANK)),
        'mla_w_uq': nrm((N_EVEN, MLA_Q_RANK, MLA_HEADS * (MLA_NOPE + MLA_ROPE)), MLA_Q_RANK ** -0.5),
        'mla_kv_norm': gain((N_EVEN, MLA_KV_RANK)),
        'mla_w_ukv': nrm((N_EVEN, MLA_KV_RANK, MLA_HEADS * (MLA_NOPE + MLA_V)), MLA_KV_RANK ** -0.5),
        'mla_qn_g': gain((N_EVEN, MLA_NOPE)),
        'mla_qr_g': gain((N_EVEN, MLA_ROPE)),
        'mla_kn_g': gain((N_EVEN, MLA_NOPE)),
        'mla_kr_g': gain((N_EVEN, MLA_ROPE)),
        'gqa_q_g': gain((N_EVEN, GQA_HEAD_DIM)),
        'gqa_k_g': gain((N_EVEN, GQA_HEAD_DIM)),
        'attn_w_out': nrm((N_EVEN, ATTN_OUT, D), ATTN_OUT ** -0.5),
        'rwkv_mix': unif((N_ODD, N_MIX, D), 0.0, 1.0),
        'rwkv_w_r': nrm((N_ODD, D, D), D ** -0.5),
        'rwkv_w_k': nrm((N_ODD, D, D), D ** -0.5),
        'rwkv_w_v': nrm((N_ODD, D, D), D ** -0.5),
        'rwkv_w_o': nrm((N_ODD, D, D), D ** -0.5),
        'rwkv_w0': unif((N_ODD, 2, D), -5.0, -1.0),
        'rwkv_w1': nrm((N_ODD, 2, D, DECAY_LORA), D ** -0.5),
        'rwkv_w2': nrm((N_ODD, 2, DECAY_LORA, D), 0.5 * DECAY_LORA ** -0.5),
        'rwkv_a0': nrm((N_ODD, 2, D), 0.1),
        'rwkv_a1': nrm((N_ODD, 2, D, ICLR_LORA), D ** -0.5),
        'rwkv_a2': nrm((N_ODD, 2, ICLR_LORA, D), 0.5 * ICLR_LORA ** -0.5),
        'rwkv_v0': nrm((N_VRES, D), 0.1),
        'rwkv_v1': nrm((N_VRES, D, VRES_LORA), D ** -0.5),
        'rwkv_v2': nrm((N_VRES, VRES_LORA, D), 0.5 * VRES_LORA ** -0.5),
        'rwkv_g1': nrm((N_ODD, D, GATE_LORA), D ** -0.5),
        'rwkv_g2': nrm((N_ODD, GATE_LORA, D), GATE_LORA ** -0.5),
        'rwkv_k_k': 0.85 + nrm((N_ODD, D), 0.02),
        'rwkv_k_a': gain((N_ODD, D)),
        'rwkv_r_k': nrm((N_ODD, H, N), 0.1),
        'rwkv_ln_w': gain((N_ODD, D)),
        'rwkv_ln_b': nrm((N_ODD, D), 0.02),
        'moe_router_w': nrm((DEPTH, D, E), D ** -0.5),
        'moe_router_b': nrm((DEPTH, E), 0.01),
        'moe_w1': nrm((DEPTH, E, D, 2 * F), D ** -0.5),
        'moe_b1': nrm((DEPTH, E, 2 * F), 0.02),
        'moe_w2': nrm((DEPTH, E, F, D), F ** -0.5),
        'moe_b2': nrm((DEPTH, E, D), 0.02),
    }


def reference(x, c, ctx, c_ctx, ada_w, ada_b, norm_mix, norm_ffn,
              attn_w_in, mla_q_norm, mla_w_uq, mla_kv_norm, mla_w_ukv, mla_qn_g, mla_qr_g, mla_kn_g, mla_kr_g,
              gqa_q_g, gqa_k_g, attn_w_out,
              rwkv_mix, rwkv_w_r, rwkv_w_k, rwkv_w_v, rwkv_w_o, rwkv_w0, rwkv_w1, rwkv_w2,
              rwkv_a0, rwkv_a1, rwkv_a2, rwkv_v0, rwkv_v1, rwkv_v2, rwkv_g1, rwkv_g2,
              rwkv_k_k, rwkv_k_a, rwkv_r_k, rwkv_ln_w, rwkv_ln_b,
              moe_router_w, moe_router_b, moe_w1, moe_b1, moe_w2, moe_b2):
    n_lat = x.shape[1]
    ang_mla = axial_angles(n_lat, MLA_ROPE)
    ang_gqa = axial_angles(n_lat, GQA_HEAD_DIM)
    cond_lat = jax.nn.silu(c)[:, None, :]
    cond_ctx = jax.nn.silu(c_ctx)[None, None, :]
    vf_lat, vf_ctx = None, None
    for i in range(DEPTH):
        with_ctx = i < DEPTH - 1
        m_lat = jnp.split(cond_lat @ ada_w[i] + ada_b[i], N_MOD, axis=-1)
        m_ctx = jnp.split(cond_ctx @ ada_w[i] + ada_b[i], N_MOD, axis=-1)
        h_lat = modulate(rmsnorm(x, norm_mix[i]), m_lat[0], m_lat[1])
        h_ctx = modulate(rmsnorm(ctx, norm_mix[i]), m_ctx[0], m_ctx[1])
        if i % 2 == 0:
            e = i // 2
            p = dict(w_in=attn_w_in[e], q_norm=mla_q_norm[e], w_uq=mla_w_uq[e], kv_norm=mla_kv_norm[e],
                     w_ukv=mla_w_ukv[e], qn_g=mla_qn_g[e], qr_g=mla_qr_g[e], kn_g=mla_kn_g[e], kr_g=mla_kr_g[e],
                     gq_g=gqa_q_g[e], gk_g=gqa_k_g[e], w_out=attn_w_out[e])
            o_lat, o_ctx = attn_mixer(h_lat, h_ctx, p, ang_mla, ang_gqa, with_ctx)
        else:
            j = i // 2
            p = dict(mix=rwkv_mix[j], w_r=rwkv_w_r[j], w_k=rwkv_w_k[j], w_v=rwkv_w_v[j], w_o=rwkv_w_o[j],
                     w0=rwkv_w0[j], w1=rwkv_w1[j], w2=rwkv_w2[j], a0=rwkv_a0[j], a1=rwkv_a1[j], a2=rwkv_a2[j],
                     g1=rwkv_g1[j], g2=rwkv_g2[j], k_k=rwkv_k_k[j], k_a=rwkv_k_a[j], r_k=rwkv_r_k[j],
                     ln_w=rwkv_ln_w[j], ln_b=rwkv_ln_b[j])
            if j > 0:
                p.update(v0=rwkv_v0[j - 1], v1=rwkv_v1[j - 1], v2=rwkv_v2[j - 1])
            o_lat, o_ctx, v_lat, v_ctx = rwkv_mixer(h_lat, h_ctx, p, vf_lat, vf_ctx, with_ctx)
            if j == 0:
                vf_lat, vf_ctx = v_lat, v_ctx
        x = x + m_lat[2] * o_lat
        ffn_lat = modulate(rmsnorm(x, norm_ffn[i]), m_lat[3], m_lat[4])
        moe_args = (moe_router_w[i], moe_router_b[i], moe_w1[i], moe_b1[i], moe_w2[i], moe_b2[i])
        if with_ctx:
            ctx = ctx + m_ctx[2] * o_ctx
            ffn_ctx = modulate(rmsnorm(ctx, norm_ffn[i]), m_ctx[3], m_ctx[4])
            n_tok = ffn_lat.shape[0] * ffn_lat.shape[1]
            tokens = jnp.concatenate([ffn_lat.reshape(-1, D_MODEL), ffn_ctx.reshape(-1, D_MODEL)], axis=0)
            out = moe(tokens, *moe_args)
            x = x + m_lat[5] * out[:n_tok].reshape(x.shape)
            ctx = ctx + m_ctx[5] * out[n_tok:].reshape(ctx.shape)
        else:
            x = x + m_lat[5] * moe(ffn_lat.reshape(-1, D_MODEL), *moe_args).reshape(x.shape)
    return x
```

```python
import functools

import jax
import jax.numpy as jnp
import numpy as np
from jax import lax
from jax.experimental import pallas as pl
from jax.experimental.pallas import tpu as pltpu

F32 = jnp.float32
BF16 = jnp.bfloat16
HIGHEST = lax.Precision.HIGHEST

SUBLANES = 8
LANES = 128
VMEM_LIMIT = 56 * 1024 * 1024

GRID_W = 64
ROPE_THETA = 10000.0
NORM_EPS = 1e-6
N_MOD = 6

MLA_HEADS = 4
MLA_Q_RANK = 384
MLA_KV_RANK = 256
MLA_NOPE = 128
MLA_ROPE = 64
MLA_V = 128
MLA_SCALE = (MLA_NOPE + MLA_ROPE) ** -0.5
GQA_HEADS = 4
GQA_KV_HEADS = 2
GQA_HEAD_DIM = 128
GQA_SCALE = GQA_HEAD_DIM ** -0.5

RWKV_HEAD = 64
RWKV_GN_EPS = 64e-5
N_MIX = 6
DECAY_SCALE = float(np.exp(-0.5))

N_EXPERTS = 32
TOP_K = 4
SWIGLU_ALPHA = 1.702
SWIGLU_LIMIT = 7.0

TM = 256
TM_FFN = 512
TT_SCAN = 32


def _params(*sem):
    return pltpu.CompilerParams(dimension_semantics=sem, vmem_limit_bytes=VMEM_LIMIT)


def _dot(a, b):
    return jnp.dot(a, b, preferred_element_type=F32)


def _bdot(a, b_ref):
    return jnp.dot(a.astype(BF16), b_ref[...], preferred_element_type=F32)


def _rms(x, g, n):
    ms = jnp.sum(x * x, axis=-1, keepdims=True) * (1.0 / n)
    return x * lax.rsqrt(ms + NORM_EPS) * g


def _sigmoid(x):
    return 1.0 / (1.0 + jnp.exp(-x))


def _rope(x, cos, sin):
    return x * cos + pltpu.roll(x, 64, axis=1) * sin


def _adaln_kernel(c_ref, w_ref, b_ref, o_ref):
    c = c_ref[...]
    s = c * _sigmoid(c)
    o_ref[0] = jnp.dot(s, w_ref[0], precision=HIGHEST, preferred_element_type=F32) + b_ref[0]


def _adaln(cond, ada_w, ada_b):
    depth, d, nd = ada_w.shape
    rows = cond.shape[0]
    return pl.pallas_call(
        _adaln_kernel,
        out_shape=jax.ShapeDtypeStruct((depth, rows, nd), F32),
        grid=(depth, nd // d),
        in_specs=[pl.BlockSpec((rows, d), lambda l, n: (0, 0)),
                  pl.BlockSpec((1, d, d), lambda l, n: (l, 0, n)),
                  pl.BlockSpec((1, 1, d), lambda l, n: (l, 0, n))],
        out_specs=pl.BlockSpec((1, rows, d), lambda l, n: (l, 0, n)),
        compiler_params=_params("arbitrary", "arbitrary"),
        name="adaln",
    )(cond, ada_w, ada_b.reshape(depth, 1, nd))


def _ffn_tail(x2, mod, nf_ref, rw_ref, rb_ref, hf_o, idx_o, p_o):
    d = x2.shape[-1]
    hf = _rms(x2, nf_ref[...], d) * (1.0 + mod[4:5]) + mod[3:4]
    hf_o[0] = hf
    logits = jnp.dot(hf, rw_ref[...], precision=HIGHEST, preferred_element_type=F32) + rb_ref[...]
    lane = lax.broadcasted_iota(jnp.int32, logits.shape, 1).astype(F32)
    vals, idxs = [], []
    l = logits
    for _ in range(TOP_K):
        m = jnp.max(l, axis=-1, keepdims=True)
        ik = jnp.min(jnp.where(l == m, lane, float(LANES)), axis=-1, keepdims=True)
        vals.append(m)
        idxs.append(ik)
        l = jnp.where(lane == ik, -jnp.inf, l)
    es = [jnp.exp(v - vals[0]) for v in vals]
    inv = 1.0 / (es[0] + es[1] + es[2] + es[3])
    idx_acc = jnp.zeros_like(logits)
    p_acc = jnp.zeros_like(logits)
    for k in range(TOP_K):
        idx_acc = jnp.where(lane == float(k), idxs[k], idx_acc)
        p_acc = jnp.where(lane == float(k), es[k] * inv, p_acc)
    idx_o[0] = idx_acc.astype(jnp.int32)
    p_o[0] = p_acc


def _attn_proj_kernel(x_ref, mod_ref, nm_ref, win_ref, qn_ref, wuq_ref, kvn_ref, wukv_ref,
                      qng_ref, qrg_ref, kng_ref, krg_ref, gqg_ref, gkg_ref,
                      cm_ref, sm_ref, cg_ref, sg_ref,
                      qm_o, km_o, vm_o, qg_o, kg_o, vg_o):
    x = x_ref[0]
    mod = mod_ref[0, 0]
    d = x.shape[-1]
    h = _rms(x, nm_ref[...], d) * (1.0 + mod[1:2]) + mod[0:1]
    z = _bdot(h, win_ref)
    o1 = MLA_Q_RANK
    o2 = o1 + MLA_KV_RANK
    o3 = o2 + LANES
    o4 = o3 + GQA_HEADS * GQA_HEAD_DIM
    o5 = o4 + GQA_KV_HEADS * GQA_HEAD_DIM
    q = _bdot(_rms(z[:, :o1], qn_ref[...], MLA_Q_RANK), wuq_ref)
    kv = _bdot(_rms(z[:, o1:o2], kvn_ref[...], MLA_KV_RANK), wukv_ref)
    cm, sm, cg, sg = cm_ref[...], sm_ref[...], cg_ref[...], sg_ref[...]
    kr = _rope(_rms(z[:, o2:o3], krg_ref[...], MLA_ROPE), cm, sm)
    for hh in range(MLA_HEADS):
        b0 = hh * 2 * LANES
        qn = _rms(q[:, b0:b0 + LANES], qng_ref[...], MLA_NOPE)
        qr = _rope(_rms(q[:, b0 + LANES:b0 + 2 * LANES], qrg_ref[...], MLA_ROPE), cm, sm)
        qm_o[0, hh] = (jnp.concatenate([qn, qr], axis=1) * MLA_SCALE).astype(BF16)
        kn = _rms(kv[:, b0:b0 + LANES], kng_ref[...], MLA_NOPE)
        km_o[0, hh] = jnp.concatenate([kn, kr], axis=1).astype(BF16)
        vm_o[0, hh] = kv[:, b0 + LANES:b0 + 2 * LANES].astype(BF16)
    for hh in range(GQA_HEADS):
        g = z[:, o3 + hh * LANES:o3 + (hh + 1) * LANES]
        qg_o[0, hh] = (_rope(_rms(g, gqg_ref[...], GQA_HEAD_DIM), cg, sg) * GQA_SCALE).astype(BF16)
    for hh in range(GQA_KV_HEADS):
        g = z[:, o4 + hh * LANES:o4 + (hh + 1) * LANES]
        kg_o[0, hh] = _rope(_rms(g, gkg_ref[...], GQA_HEAD_DIM), cg, sg).astype(BF16)
        vg_o[0, hh] = z[:, o5 + hh * LANES:o5 + (hh + 1) * LANES].astype(BF16)


def _full(shape):
    nd = len(shape)
    return pl.BlockSpec(shape, lambda b, i: (0,) * nd)


def _mod_spec(d):
    return pl.BlockSpec((1, 1, N_MOD, d), lambda b, i: (b, jnp.minimum(i, 1), 0, 0))


def _row_spec(d):
    return pl.BlockSpec((1, TM, d), lambda b, i: (b, i, 0))


def _attn_proj(x, mod, nm, aw, tabs):
    bsz, tu, d = x.shape
    nt = tu // TM
    head_spec = lambda nh, w: pl.BlockSpec((1, nh, TM, w), lambda b, i: (b, 0, i, 0))
    tab_spec = pl.BlockSpec((TM, LANES), lambda b, i: (i, 0))
    weights = [nm, aw["w_in"], aw["q_norm"], aw["w_uq"], aw["kv_norm"], aw["w_ukv"],
               aw["qn_g"], aw["qr_g"], aw["kn_g"], aw["kr_g"], aw["gq_g"], aw["gk_g"]]
    out_shape = [
        jax.ShapeDtypeStruct((bsz, MLA_HEADS, tu, 2 * LANES), BF16),
        jax.ShapeDtypeStruct((bsz, MLA_HEADS, tu, 2 * LANES), BF16),
        jax.ShapeDtypeStruct((bsz, MLA_HEADS, tu, LANES), BF16),
        jax.ShapeDtypeStruct((bsz, GQA_HEADS, tu, LANES), BF16),
        jax.ShapeDtypeStruct((bsz, GQA_KV_HEADS, tu, LANES), BF16),
        jax.ShapeDtypeStruct((bsz, GQA_KV_HEADS, tu, LANES), BF16),
    ]
    return pl.pallas_call(
        _attn_proj_kernel,
        out_shape=out_shape,
        grid=(bsz, nt),
        in_specs=[_row_spec(d), _mod_spec(d)] + [_full(w.shape) for w in weights] + [tab_spec] * 4,
        out_specs=[head_spec(MLA_HEADS, 2 * LANES), head_spec(MLA_HEADS, 2 * LANES),
                   head_spec(MLA_HEADS, LANES), head_spec(GQA_HEADS, LANES),
                   head_spec(GQA_KV_HEADS, LANES), head_spec(GQA_KV_HEADS, LANES)],
        compiler_params=_params("parallel", "parallel"),
        name="attn_proj",
    )(x, mod, *weights, *tabs)


def _attn_kernel(n_ctx, q_ref, k_ref, v_ref, o_ref):
    q = q_ref[0, 0]

    def attend(k, v):
        s = lax.dot_general(q, k, (((1,), (1,)), ((), ())), preferred_element_type=F32)
        m = jnp.max(s, axis=-1, keepdims=True)
        p = jnp.exp(s - m)
        l = jnp.sum(p, axis=-1, keepdims=True)
        o = _dot(p.astype(BF16), v)
        o_ref[0] = (o * (1.0 / l)).astype(o_ref.dtype)

    is_ctx = pl.program_id(2) == 0

    @pl.when(is_ctx)
    def _():
        attend(k_ref[0, 0, :n_ctx], v_ref[0, 0, :n_ctx])

    @pl.when(jnp.logical_not(is_ctx))
    def _():
        attend(k_ref[0, 0], v_ref[0, 0])


def _attention(q, k, v, n_ctx):
    bsz, hq, tu, dk = q.shape
    hk = k.shape[1]
    grp = hq // hk
    dv = v.shape[-1]
    return pl.pallas_call(
        functools.partial(_attn_kernel, n_ctx),
        out_shape=jax.ShapeDtypeStruct((bsz, tu, hq * dv), BF16),
        grid=(bsz, hq, tu // TM),
        in_specs=[pl.BlockSpec((1, 1, TM, dk), lambda b, h, i: (b, h, i, 0)),
                  pl.BlockSpec((1, 1, tu, dk), lambda b, h, i: (b, h // grp, 0, 0)),
                  pl.BlockSpec((1, 1, tu, dv), lambda b, h, i: (b, h // grp, 0, 0))],
        out_specs=pl.BlockSpec((1, TM, dv), lambda b, h, i: (b, i, h)),
        compiler_params=_params("parallel", "parallel", "arbitrary"),
        name="attention",
    )(q, k, v)


def _attn_out_kernel(om_ref, og_ref, x_ref, mod_ref, wo_ref, nf_ref, rw_ref, rb_ref,
                     x_o, hf_o, idx_o, p_o):
    mod = mod_ref[0, 0]
    nm = om_ref.shape[-1]
    o = _dot(om_ref[0], wo_ref[:nm]) + _dot(og_ref[0], wo_ref[nm:])
    x2 = x_ref[0] + mod[2:3] * o
    x_o[0] = x2
    _ffn_tail(x2, mod, nf_ref, rw_ref, rb_ref, hf_o, idx_o, p_o)


def _tail_out_shapes(bsz, tu, d):
    return [jax.ShapeDtypeStruct((bsz, tu, d), F32), jax.ShapeDtypeStruct((bsz, tu, d), F32),
            jax.ShapeDtypeStruct((bsz, tu, LANES), jnp.int32), jax.ShapeDtypeStruct((bsz, tu, LANES), F32)]


def _tail_out_specs(d):
    return [_row_spec(d), _row_spec(d), _row_spec(LANES), _row_spec(LANES)]


def _attn_out(om, og, x, mod, wo, nf, rw, rb):
    bsz, tu, d = x.shape
    return pl.pallas_call(
        _attn_out_kernel,
        out_shape=_tail_out_shapes(bsz, tu, d),
        grid=(bsz, tu // TM),
        in_specs=[_row_spec(om.shape[-1]), _row_spec(og.shape[-1]), _row_spec(d), _mod_spec(d),
                  _full(wo.shape), _full(nf.shape), _full(rw.shape), _full(rb.shape)],
        out_specs=_tail_out_specs(d),
        compiler_params=_params("parallel", "parallel"),
        name="attn_out",
    )(om, og, x, mod, wo, nf, rw, rb)


def _head_sum(x, hs1_ref, hs2_ref):
    hi = x.astype(BF16)
    lo = (x - hi.astype(F32)).astype(BF16)
    s = _dot(hi, hs1_ref[...]) + _dot(lo, hs1_ref[...])
    shi = s.astype(BF16)
    slo = (s - shi.astype(F32)).astype(BF16)
    return _dot(shi, hs2_ref[...]) + _dot(slo, hs2_ref[...])


def _rwkv_feat_kernel(has_vres, nt, x_ref, xp_ref, xn_ref, mod_ref, nm_ref, mix_ref,
                      wr_ref, wk_ref, wv_ref, w0_ref, w1_ref, w2_ref, a0_ref, a1_ref, a2_ref,
                      g1_ref, g2_ref, kk_ref, ka_ref, rk_ref, hs1_ref, hs2_ref, *rest):
    if has_vres:
        v0_ref, v1_ref, v2_ref, vf_ref = rest[:4]
        rest = rest[4:]
    r_o, v_o, kk_o, w_o, k_o, b_o, g_o, bv_o = rest
    i = pl.program_id(1)
    mod = mod_ref[0, 0]
    d = x_ref.shape[-1]
    nm = nm_ref[...]

    def hmod(xx):
        return _rms(xx, nm, d) * (1.0 + mod[1:2]) + mod[0:1]

    h = hmod(x_ref[0])
    has_prev = (i >= 2).astype(F32)
    has_next = jnp.logical_and(i >= 1, i < nt - 1).astype(F32)
    prev_row = hmod(xp_ref[0])[SUBLANES - 1:SUBLANES] * has_prev
    next_row = hmod(xn_ref[0])[0:1] * has_next
    row = lax.broadcasted_iota(jnp.int32, h.shape, 0)
    hp = jnp.where(row == 0, prev_row, pltpu.roll(h, 1, axis=0))
    hn = jnp.where(row == TM - 1, next_row, pltpu.roll(h, TM - 1, axis=0))
    xx = 0.5 * (hp + hn) - h
    xr, xw, xk, xv, xa, xg = (h + xx * mix_ref[m:m + 1] for m in range(N_MIX))

    r = _bdot(xr, wr_ref)
    k = _bdot(xk, wk_ref)
    v = _bdot(xv, wv_ref)
    if has_vres:
        gate = _sigmoid(v0_ref[...] + _bdot(_bdot(xv, v1_ref), v2_ref))
        v = v + (vf_ref[0] - v) * gate
    kk = k * kk_ref[...]
    kk = kk * lax.rsqrt(jnp.maximum(_head_sum(kk * kk, hs1_ref, hs2_ref), 1e-24))
    tw = jnp.tanh(_bdot(xw, w1_ref)).astype(BF16)
    ta = _bdot(xa, a1_ref).astype(BF16)
    ksum = jnp.zeros_like(k)
    for dd in range(2):
        z = w0_ref[dd:dd + 1] + _dot(tw, w2_ref[dd])
        w_o[dd, 0] = jnp.exp(-DECAY_SCALE * _sigmoid(z))
        a = _sigmoid(a0_ref[dd:dd + 1] + _dot(ta, a2_ref[dd]))
        kd = k * (1.0 + (a - 1.0) * ka_ref[...])
        k_o[dd, 0] = kd
        b_o[dd, 0] = kk * a
        ksum = ksum + kd
    g_o[0] = _bdot(_sigmoid(_bdot(xg, g1_ref)), g2_ref)
    bv_o[0] = _head_sum(r * ksum * rk_ref[...], hs1_ref, hs2_ref) * v
    r_o[0] = r
    v_o[0] = v
    kk_o[0] = kk


def _rwkv_feat(x, mod, nm, rp, v_first):
    bsz, tu, d = x.shape
    nt = tu // TM
    has_vres = v_first is not None
    nblk8 = tu // SUBLANES
    per8 = TM // SUBLANES
    prev_spec = pl.BlockSpec((1, SUBLANES, d), lambda b, i: (b, jnp.maximum(i * per8 - 1, 0), 0))
    next_spec = pl.BlockSpec((1, SUBLANES, d), lambda b, i: (b, jnp.minimum((i + 1) * per8, nblk8 - 1), 0))
    weights = [nm, rp["mix"], rp["w_r"], rp["w_k"], rp["w_v"], rp["w0"], rp["w1"], rp["w2"],
               rp["a0"], rp["a1"], rp["a2"], rp["g1"], rp["g2"], rp["k_k"], rp["k_a"], rp["r_k"],
               rp["hs1"], rp["hs2"]]
    extra, extra_specs = [], []
    if has_vres:
        extra = [rp["v0"], rp["v1"], rp["v2"], v_first]
        extra_specs = [_full(rp["v0"].shape), _full(rp["v1"].shape), _full(rp["v2"].shape), _row_spec(d)]
    one = jax.ShapeDtypeStruct((bsz, tu, d), F32)
    two = jax.ShapeDtypeStruct((2, bsz, tu, d), F32)
    dir_spec = pl.BlockSpec((2, 1, TM, d), lambda b, i: (0, b, i, 0))
    return pl.pallas_call(
        functools.partial(_rwkv_feat_kernel, has_vres, nt),
        out_shape=[one, one, one, two, two, two, one, one],
        grid=(bsz, nt),
        in_specs=[_row_spec(d), prev_spec, next_spec, _mod_spec(d)] + [_full(w.shape) for w in weights] + extra_specs,
        out_specs=[_row_spec(d)] * 3 + [dir_spec] * 3 + [_row_spec(d)] * 2,
        compiler_params=_params("parallel", "parallel"),
        name="rwkv_feat",
    )(x, x, x, mod, *weights, *extra)


def _wkv_kernel(r_ref, v_ref, kk_ref, w_ref, k_ref, b_ref, y_ref, s_scr, wr_scr, br_scr, kr_scr):
    dirn = pl.program_id(0)
    j = pl.program_id(1)
    tt, n, lanes = r_ref.shape
    grp = n // SUBLANES

    @pl.when(j == 0)
    def _():
        s_scr[...] = jnp.zeros_like(s_scr)

    r_all = r_ref[...]
    wr_scr[...] = w_ref[0] * r_all
    br_scr[...] = jnp.sum(b_ref[0] * r_all, axis=1)
    kr_scr[...] = jnp.sum(k_ref[0] * r_all, axis=1)

    def bc(ref, t, kx):
        return jnp.broadcast_to(ref[t, pl.ds(kx, 1), :].reshape(1, 1, lanes), (grp, SUBLANES, lanes))

    def bc0(ref, t, kx):
        return jnp.broadcast_to(ref[0, t, pl.ds(kx, 1), :].reshape(1, 1, lanes), (grp, SUBLANES, lanes))

    def step(it, carry):
        t = jnp.where(dirn == 0, it, tt - 1 - it)
        sa = jnp.zeros((grp, SUBLANES, lanes), F32)
        yp = jnp.zeros((grp, SUBLANES, lanes), F32)
        for kx in range(n):
            sk = s_scr[kx].reshape(grp, SUBLANES, lanes)
            sa = sa - sk * bc(kk_ref, t, kx)
            yp = yp + sk * bc(wr_scr, t, kx)
        vt = v_ref[t].reshape(grp, SUBLANES, lanes)
        for kx in range(n):
            sk = s_scr[kx].reshape(grp, SUBLANES, lanes)
            new = sk * bc0(w_ref, t, kx) + sa * bc0(b_ref, t, kx) + vt * bc0(k_ref, t, kx)
            s_scr[kx] = new.reshape(n, lanes)
        br = br_scr[pl.ds(t, 1), :].reshape(1, 1, lanes)
        kr = kr_scr[pl.ds(t, 1), :].reshape(1, 1, lanes)
        y = yp + sa * br + vt * kr
        y_ref[0, t] = y.reshape(n, lanes)
        return carry

    lax.fori_loop(0, tt, step, 0)


def _wkv_scan(r, v, kk, w, k, b, n_ctx):
    tu, n, lanes = r.shape
    nb = tu // TT_SCAN
    cb = n_ctx // TT_SCAN

    def blk(dd, j):
        back = jnp.where(j < cb, cb - 1 - j, nb - 1 + cb - j)
        return jnp.where(dd == 0, j, back)

    shared = pl.BlockSpec((TT_SCAN, n, lanes), lambda dd, j: (blk(dd, j), 0, 0))
    per_dir = pl.BlockSpec((1, TT_SCAN, n, lanes), lambda dd, j: (dd, blk(dd, j), 0, 0))
    return pl.pallas_call(
        _wkv_kernel,
        out_shape=jax.ShapeDtypeStruct((2, tu, n, lanes), F32),
        grid=(2, nb),
        in_specs=[shared, shared, shared, per_dir, per_dir, per_dir],
        out_specs=per_dir,
        scratch_shapes=[pltpu.VMEM((n, n, lanes), F32), pltpu.VMEM((TT_SCAN, n, lanes), F32),
                        pltpu.VMEM((TT_SCAN, lanes), F32), pltpu.VMEM((TT_SCAN, lanes), F32)],
        compiler_params=_params("arbitrary", "arbitrary"),
        name="wkv_scan",
    )(r, v, kk, w, k, b)


def _rwkv_out_kernel(y_ref, g_ref, bv_ref, x_ref, mod_ref, lnw_ref, lnb_ref, wo_ref, hs1_ref, hs2_ref,
                     nf_ref, rw_ref, rb_ref, x_o, hf_o, idx_o, p_o):
    mod = mod_ref[0, 0]
    y = y_ref[0]
    inv_n = 1.0 / RWKV_HEAD
    mu = _head_sum(y, hs1_ref, hs2_ref) * inv_n
    dlt = y - mu
    var = _head_sum(dlt * dlt, hs1_ref, hs2_ref) * inv_n
    yn = dlt * lax.rsqrt(var + RWKV_GN_EPS) * lnw_ref[...] + lnb_ref[...]
    o = _bdot((yn + bv_ref[0]) * g_ref[0], wo_ref)
    x2 = x_ref[0] + mod[2:3] * o
    x_o[0] = x2
    _ffn_tail(x2, mod, nf_ref, rw_ref, rb_ref, hf_o, idx_o, p_o)


def _rwkv_out(y, g, bv, x, mod, rp, nf, rw, rb):
    bsz, tu, d = x.shape
    weights = [rp["ln_w"], rp["ln_b"], rp["w_o"], rp["hs1"], rp["hs2"], nf, rw, rb]
    return pl.pallas_call(
        _rwkv_out_kernel,
        out_shape=_tail_out_shapes(bsz, tu, d),
        grid=(bsz, tu // TM),
        in_specs=[_row_spec(d)] * 4 + [_mod_spec(d)] + [_full(w.shape) for w in weights],
        out_specs=_tail_out_specs(d),
        compiler_params=_params("parallel", "parallel"),
        name="rwkv_out",
    )(y, g, bv, x, mod, *weights)


def _dispatch_kernel(pos_ref, h_hbm, init_hbm, out_hbm, sem):
    del init_hbm
    tok0 = pl.program_id(0) * TM

    def issue(jj, carry):
        for kk in range(TOP_K):
            pltpu.make_async_copy(h_hbm.at[tok0 + jj], out_hbm.at[pos_ref[jj * TOP_K + kk]], sem).start()
        return carry

    lax.fori_loop(0, TM, issue, 0)
    n = TM * TOP_K
    pltpu.make_async_copy(h_hbm.at[pl.ds(0, n)], out_hbm.at[pl.ds(0, n)], sem).wait()


def _dispatch(pos_flat, h3, n_rows):
    n_tok = h3.shape[0]
    init = jnp.zeros((n_rows,) + h3.shape[1:], h3.dtype)
    return pl.pallas_call(
        _dispatch_kernel,
        out_shape=jax.ShapeDtypeStruct(init.shape, init.dtype),
        grid=(n_tok // TM,),
        in_specs=[pl.BlockSpec((TM * TOP_K,), lambda i: (i,), memory_space=pltpu.SMEM),
                  pl.BlockSpec(memory_space=pl.ANY), pl.BlockSpec(memory_space=pl.ANY)],
        out_specs=pl.BlockSpec(memory_space=pl.ANY),
        scratch_shapes=[pltpu.SemaphoreType.DMA(())],
        input_output_aliases={2: 0},
        compiler_params=_params("arbitrary"),
        name="moe_dispatch",
    )(pos_flat, h3, init)


def _rows_from_tiles(ref):
    return jnp.concatenate([ref[:, s, :] for s in range(SUBLANES)], axis=1)


def _moe_ffn_kernel(te_ref, nu_ref, hs_ref, w1_ref, b1_ref, w2_ref, b2_ref, ys_ref):
    t = pl.program_id(0)

    @pl.when(t < nu_ref[0])
    def _():
        x = _rows_from_tiles(hs_ref).astype(BF16)
        u = _dot(x, w1_ref[0]) + b1_ref[0]
        f = u.shape[-1] // 2
        glu = jnp.minimum(u[:, :f], SWIGLU_LIMIT)
        lin = jnp.clip(u[:, f:], -SWIGLU_LIMIT, SWIGLU_LIMIT)
        act = glu * _sigmoid(SWIGLU_ALPHA * glu) * (lin + 1.0)
        y = _dot(act.astype(BF16), w2_ref[0]) + b2_ref[0]
        for s in range(SUBLANES):
            ys_ref[:, s, :] = y[:, s * LANES:(s + 1) * LANES]

    @pl.when(t >= nu_ref[0])
    def _():
        ys_ref[...] = jnp.zeros_like(ys_ref)


def _moe_ffn(tile_expert, n_used, hs3, w1, b1, w2, b2):
    n_rows = hs3.shape[0]
    n_tiles = n_rows // TM_FFN
    d = w1.shape[1]
    f2 = w1.shape[2]
    row_map = lambda t, te, nu: (jnp.minimum(t, nu[0] - 1), 0, 0)
    exp_map = lambda t, te, nu: (te[t], 0, 0)
    grid_spec = pltpu.PrefetchScalarGridSpec(
        num_scalar_prefetch=2,
        grid=(n_tiles,),
        in_specs=[pl.BlockSpec((TM_FFN, SUBLANES, LANES), row_map),
                  pl.BlockSpec((1, d, f2), exp_map), pl.BlockSpec((1, 1, f2), exp_map),
                  pl.BlockSpec((1, f2 // 2, d), exp_map), pl.BlockSpec((1, 1, d), exp_map)],
        out_specs=pl.BlockSpec((TM_FFN, SUBLANES, LANES), lambda t, te, nu: (t, 0, 0)),
    )
    return pl.pallas_call(
        _moe_ffn_kernel,
        out_shape=jax.ShapeDtypeStruct(hs3.shape, F32),
        grid_spec=grid_spec,
        compiler_params=_params("arbitrary"),
        name="moe_ffn",
    )(tile_expert, n_used, hs3, w1, b1, w2, b2)


def _combine_kernel(pos_ref, ys_hbm, x_ref, p_ref, mod_ref, x_o, ybuf, sem):
    def issue(jj, carry):
        for kk in range(TOP_K):
            pltpu.make_async_copy(ys_hbm.at[pos_ref[jj * TOP_K + kk]], ybuf.at[kk * TM + jj], sem).start()
        return carry

    lax.fori_loop(0, TM, issue, 0)
    n = TM * TOP_K
    pltpu.make_async_copy(ys_hbm.at[pl.ds(0, n)], ybuf, sem).wait()
    p = p_ref[0]
    acc = jnp.zeros(x_ref.shape[1:], F32)
    for kk in range(TOP_K):
        yk = jnp.concatenate([ybuf[pl.ds(kk * TM, TM), s, :] for s in range(SUBLANES)], axis=1)
        acc = acc + p[:, kk:kk + 1] * yk
    x_o[0] = x_ref[0] + mod_ref[0, 0][5:6] * acc


def _combine(pos_flat, ys3, x, p_pad, mod):
    bsz, tu, d = x.shape
    nt = tu // TM
    return pl.pallas_call(
        _combine_kernel,
        out_shape=jax.ShapeDtypeStruct(x.shape, F32),
        grid=(bsz, nt),
        in_specs=[pl.BlockSpec((TM * TOP_K,), lambda b, i: (b * nt + i,), memory_space=pltpu.SMEM),
                  pl.BlockSpec(memory_space=pl.ANY), _row_spec(d), _row_spec(LANES), _mod_spec(d)],
        out_specs=_row_spec(d),
        scratch_shapes=[pltpu.VMEM((TM * TOP_K, SUBLANES, LANES), F32), pltpu.SemaphoreType.DMA(())],
        compiler_params=_params("arbitrary", "arbitrary"),
        name="moe_combine",
    )(pos_flat, ys3, x, p_pad, mod)


def _moe(x, hf, idx_pad, p_pad, mod, w1, b1, w2, b2):
    bsz, tu, d = x.shape
    n_tok = bsz * tu
    n_rows = n_tok * TOP_K + N_EXPERTS * TM_FFN
    n_tiles = n_rows // TM_FFN
    idx = idx_pad.reshape(n_tok, LANES)[:, :TOP_K]
    onehot = jnp.sum((idx[:, :, None] == jnp.arange(N_EXPERTS, dtype=jnp.int32)).astype(jnp.int32), axis=1)
    csum = jnp.cumsum(onehot, axis=0)
    counts = csum[-1]
    padded = ((counts + TM_FFN - 1) // TM_FFN) * TM_FFN
    ends = jnp.cumsum(padded)
    pos = (ends - padded)[idx] + jnp.take_along_axis(csum - onehot, idx, axis=1)
    n_used = (ends[-1] // TM_FFN).astype(jnp.int32)
    tile_start = jnp.arange(n_tiles, dtype=jnp.int32) * TM_FFN
    tile_expert = jnp.sum((tile_start[:, None] >= ends[None, :]).astype(jnp.int32), axis=1)
    last = jnp.take(tile_expert, jnp.maximum(n_used - 1, 0))
    tile_expert = jnp.where(jnp.arange(n_tiles) < n_used, tile_expert, last).astype(jnp.int32)
    pos_flat = pos.reshape(-1).astype(jnp.int32)

    h3 = hf.reshape(n_tok, SUBLANES, LANES)
    hs3 = _dispatch(pos_flat, h3, n_rows)
    ys3 = _moe_ffn(tile_expert, n_used.reshape(1), hs3, w1, b1, w2, b2)
    return _combine(pos_flat, ys3, x, p_pad, mod)


def _rope_tables(n_ctx, n_lat):
    def angles(rot_dim):
        rows = n_lat // GRID_W
        row = jnp.repeat(jnp.arange(rows, dtype=F32), GRID_W)
        col = jnp.tile(jnp.arange(GRID_W, dtype=F32), rows)
        half = rot_dim // 2
        inv_freq = ROPE_THETA ** (-jnp.arange(0, half, 2, dtype=F32) / half)
        return row[:, None] * inv_freq[None, :], col[:, None] * inv_freq[None, :]

    def with_ctx(t, fill):
        return jnp.concatenate([jnp.full((n_ctx, LANES), fill, F32), t], axis=0)

    ar, ac = angles(MLA_ROPE)
    one = jnp.ones((n_lat, 32), F32)
    zero = jnp.zeros((n_lat, 32), F32)
    cm = jnp.concatenate([jnp.cos(ar), jnp.cos(ac), one, jnp.cos(ar), jnp.cos(ac), one], axis=1)
    sm = jnp.concatenate([-jnp.sin(ar), -jnp.sin(ac), zero, jnp.sin(ar), jnp.sin(ac), zero], axis=1)
    ar, ac = angles(GQA_HEAD_DIM)
    cg = jnp.concatenate([jnp.cos(ar), jnp.cos(ac), jnp.cos(ar), jnp.cos(ac)], axis=1)
    sg = jnp.concatenate([-jnp.sin(ar), -jnp.sin(ac), jnp.sin(ar), jnp.sin(ac)], axis=1)
    return with_ctx(cm, 1.0), with_ctx(sm, 0.0), with_ctx(cg, 1.0), with_ctx(sg, 0.0)


_PERM_GQA = np.concatenate([np.arange(0, 32), np.arange(64, 96), np.arange(32, 64), np.arange(96, 128)])
_SRC_MLA = np.zeros(LANES, np.int32)
_MASK_MLA = np.zeros(LANES, np.float32)
_SRC_MLA[0:16], _SRC_MLA[16:32], _SRC_MLA[64:80], _SRC_MLA[80:96] = (
    np.arange(0, 16), np.arange(32, 48), np.arange(16, 32), np.arange(48, 64))
_MASK_MLA[0:32] = 1.0
_MASK_MLA[64:96] = 1.0


def _pad_rope64(w):
    return jnp.take(w, _SRC_MLA, axis=-1) * _MASK_MLA


def _attn_weights(e, w_in, q_norm, w_uq, kv_norm, w_ukv, qn_g, qr_g, kn_g, kr_g, gq_g, gk_g, w_out):
    s1 = MLA_Q_RANK
    s2 = s1 + MLA_KV_RANK
    s3 = s2 + MLA_ROPE
    s4 = s3 + GQA_HEADS * GQA_HEAD_DIM
    s5 = s4 + GQA_KV_HEADS * GQA_HEAD_DIM
    wi = w_in[e]
    d = wi.shape[0]
    gq = wi[:, s3:s4].reshape(d, GQA_HEADS, GQA_HEAD_DIM)[:, :, _PERM_GQA].reshape(d, -1)
    gk = wi[:, s4:s5].reshape(d, GQA_KV_HEADS, GQA_HEAD_DIM)[:, :, _PERM_GQA].reshape(d, -1)
    w_in_p = jnp.concatenate([wi[:, :s2], _pad_rope64(wi[:, s2:s3]), gq, gk, wi[:, s5:]], axis=1)
    uq = w_uq[e].reshape(MLA_Q_RANK, MLA_HEADS, MLA_NOPE + MLA_ROPE)
    uq_p = jnp.concatenate([uq[:, :, :MLA_NOPE], _pad_rope64(uq[:, :, MLA_NOPE:])], axis=-1)
    row = lambda g: g.reshape(1, -1).astype(F32)
    return dict(
        w_in=w_in_p.astype(BF16), q_norm=row(q_norm[e]), w_uq=uq_p.reshape(MLA_Q_RANK, -1).astype(BF16),
        kv_norm=row(kv_norm[e]), w_ukv=w_ukv[e].astype(BF16),
        qn_g=row(qn_g[e]), qr_g=row(_pad_rope64(qr_g[e])), kn_g=row(kn_g[e]), kr_g=row(_pad_rope64(kr_g[e])),
        gq_g=row(gq_g[e][_PERM_GQA]), gk_g=row(gk_g[e][_PERM_GQA]), w_out=w_out[e].astype(BF16))


def _pad_cols(w, n):
    return jnp.pad(w, ((0, 0), (0, n - w.shape[1])))


def _pad_rows(w, n):
    return jnp.pad(w, ((0, n - w.shape[0]), (0, 0)))


def _lora_pair(w1, w2):
    rank = w1.shape[-1]
    down = jnp.concatenate([w1[0], w1[1]], axis=1)
    up = jnp.stack([jnp.pad(w2[0], ((0, rank), (0, 0))), jnp.pad(w2[1], ((rank, 0), (0, 0)))])
    return down.astype(BF16), up.astype(BF16)


def _rwkv_weights(j, mix, w_r, w_k, w_v, w_o, w0, w1, w2, a0, a1, a2, v0, v1, v2, g1, g2, k_k, k_a, r_k, ln_w, ln_b):
    d = w_r.shape[1]
    row = lambda g: g.reshape(1, -1).astype(F32)
    heads = np.arange(d) // RWKV_HEAD
    hs1 = (heads[:, None] == np.arange(LANES)[None, :]).astype(np.float32)
    w1c, w2p = _lora_pair(w1[j], w2[j])
    a1c, a2p = _lora_pair(a1[j], a2[j])
    rp = dict(mix=mix[j], w_r=w_r[j].astype(BF16), w_k=w_k[j].astype(BF16), w_v=w_v[j].astype(BF16),
              w_o=w_o[j].astype(BF16), w0=w0[j], w1=w1c, w2=w2p, a0=a0[j], a1=a1c, a2=a2p,
              g1=g1[j].astype(BF16), g2=g2[j].astype(BF16), k_k=row(k_k[j]), k_a=row(k_a[j]), r_k=row(r_k[j]),
              ln_w=row(ln_w[j]), ln_b=row(ln_b[j]),
              hs1=jnp.asarray(hs1, BF16), hs2=jnp.asarray(hs1.T, BF16))
    if j > 0:
        rp.update(v0=row(v0[j - 1]), v1=_pad_cols(v1[j - 1], LANES).astype(BF16),
                  v2=_pad_rows(v2[j - 1], LANES).astype(BF16))
    return rp


def _to_scan(t):
    lead = t.shape[:-3]
    bsz, tu, d = t.shape[-3:]
    nl = len(lead)
    t = t.reshape(*lead, bsz, tu, d // RWKV_HEAD, RWKV_HEAD)
    perm = tuple(range(nl)) + (nl + 1, nl + 3, nl + 0, nl + 2)
    return t.transpose(perm).reshape(*lead, tu, RWKV_HEAD, bsz * (d // RWKV_HEAD))


def kernel(x, c, ctx, c_ctx, ada_w, ada_b, norm_mix, norm_ffn, attn_w_in, mla_q_norm, mla_w_uq, mla_kv_norm, mla_w_ukv, mla_qn_g, mla_qr_g, mla_kn_g, mla_kr_g, gqa_q_g, gqa_k_g, attn_w_out, rwkv_mix, rwkv_w_r, rwkv_w_k, rwkv_w_v, rwkv_w_o, rwkv_w0, rwkv_w1, rwkv_w2, rwkv_a0, rwkv_a1, rwkv_a2, rwkv_v0, rwkv_v1, rwkv_v2, rwkv_g1, rwkv_g2, rwkv_k_k, rwkv_k_a, rwkv_r_k, rwkv_ln_w, rwkv_ln_b, moe_router_w, moe_router_b, moe_w1, moe_b1, moe_w2, moe_b2):
    bsz, n_lat, d = x.shape
    n_ctx = ctx.shape[1]
    depth = ada_w.shape[0]
    tu = n_ctx + n_lat
    assert n_ctx % TM == 0 and n_lat % TM == 0 and n_ctx % TT_SCAN == 0 and d == SUBLANES * LANES
    assert (bsz * tu * TOP_K) % TM_FFN == 0

    rows = -(-(bsz + 1) // SUBLANES) * SUBLANES
    cond = jnp.zeros((rows, d), F32).at[:bsz].set(c).at[bsz].set(c_ctx)
    mods = _adaln(cond, ada_w, ada_b)
    mods_lat = mods[:, :bsz].reshape(depth, bsz, 1, N_MOD, d)
    mods_ctx = jnp.broadcast_to(mods[:, bsz].reshape(depth, 1, 1, N_MOD, d), mods_lat.shape)
    mods = jnp.concatenate([mods_ctx, mods_lat], axis=2)

    tabs = _rope_tables(n_ctx, n_lat)
    xs = jnp.concatenate([ctx, x], axis=1)

    n_e, _, f2 = moe_w1.shape[1:]
    rw_all = jnp.pad(moe_router_w, ((0, 0), (0, 0), (0, LANES - n_e)))
    rb_all = jnp.pad(moe_router_b, ((0, 0), (0, LANES - n_e)), constant_values=-1e30).reshape(depth, 1, LANES)

    v_first = None
    for i in range(depth):
        mod = mods[i]
        nm = norm_mix[i].reshape(1, d)
        nf = norm_ffn[i].reshape(1, d)
        rw, rb = rw_all[i], rb_all[i]
        if i % 2 == 0:
            aw = _attn_weights(i // 2, attn_w_in, mla_q_norm, mla_w_uq, mla_kv_norm, mla_w_ukv, mla_qn_g,
                               mla_qr_g, mla_kn_g, mla_kr_g, gqa_q_g, gqa_k_g, attn_w_out)
            qm, km, vm, qg, kg, vg = _attn_proj(xs, mod, nm, aw, tabs)
            om = _attention(qm, km, vm, n_ctx)
            og = _attention(qg, kg, vg, n_ctx)
            xs, hf, idx_pad, p_pad = _attn_out(om, og, xs, mod, aw["w_out"], nf, rw, rb)
        else:
            j = i // 2
            rp = _rwkv_weights(j, rwkv_mix, rwkv_w_r, rwkv_w_k, rwkv_w_v, rwkv_w_o, rwkv_w0, rwkv_w1, rwkv_w2,
                               rwkv_a0, rwkv_a1, rwkv_a2, rwkv_v0, rwkv_v1, rwkv_v2, rwkv_g1, rwkv_g2,
                               rwkv_k_k, rwkv_k_a, rwkv_r_k, rwkv_ln_w, rwkv_ln_b)
            r, v, kk, w, k, b, g, bv = _rwkv_feat(xs, mod, nm, rp, v_first if j > 0 else None)
            if j == 0:
                v_first = v
            y = _wkv_scan(_to_scan(r), _to_scan(v), _to_scan(kk), _to_scan(w), _to_scan(k), _to_scan(b), n_ctx)
            y = (y[0] + y[1]).reshape(tu, RWKV_HEAD, bsz, d // RWKV_HEAD).transpose(2, 0, 3, 1).reshape(bsz, tu, d)
            xs, hf, idx_pad, p_pad = _rwkv_out(y, g, bv, xs, mod, rp, nf, rw, rb)
        w1 = jnp.concatenate([moe_w1[i][:, :, 0::2], moe_w1[i][:, :, 1::2]], axis=-1).astype(BF16)
        b1 = jnp.concatenate([moe_b1[i][:, 0::2], moe_b1[i][:, 1::2]], axis=-1).reshape(n_e, 1, f2)
        w2 = moe_w2[i].astype(BF16)
        b2 = moe_b2[i].reshape(n_e, 1, d)
        xs = _moe(xs, hf, idx_pad, p_pad, mod, w1, b1, w2, b2)
    return xs[:, n_ctx:]
```

```python
import functools

import jax
import jax.numpy as jnp
import numpy as np
from jax import lax
from jax.experimental import pallas as pl
from jax.experimental.pallas import tpu as pltpu

F32 = jnp.float32
BF16 = jnp.bfloat16
HIGHEST = lax.Precision.HIGHEST

SUBLANES = 8
LANES = 128
VMEM_LIMIT = 56 * 1024 * 1024

GRID_W = 64
ROPE_THETA = 10000.0
NORM_EPS = 1e-6
N_MOD = 6

MLA_HEADS = 4
MLA_Q_RANK = 384
MLA_KV_RANK = 256
MLA_NOPE = 128
MLA_ROPE = 64
MLA_V = 128
MLA_SCALE = (MLA_NOPE + MLA_ROPE) ** -0.5
GQA_HEADS = 4
GQA_KV_HEADS = 2
GQA_HEAD_DIM = 128
GQA_SCALE = GQA_HEAD_DIM ** -0.5

RWKV_HEAD = 64
RWKV_GN_EPS = 64e-5
N_MIX = 6
DECAY_SCALE = float(np.exp(-0.5))

N_EXPERTS = 32
TOP_K = 4
SWIGLU_ALPHA = 1.702
SWIGLU_LIMIT = 7.0

TM = 256
TM_FFN = 512
TT_SCAN = 32


def _params(*sem):
    return pltpu.CompilerParams(dimension_semantics=sem, vmem_limit_bytes=VMEM_LIMIT)


def _dot(a, b):
    return jnp.dot(a, b, preferred_element_type=F32)


def _bdot(a, b_ref):
    return jnp.dot(a.astype(BF16), b_ref[...], preferred_element_type=F32)


def _rms(x, g, n):
    ms = jnp.sum(x * x, axis=-1, keepdims=True) * (1.0 / n)
    return x * lax.rsqrt(ms + NORM_EPS) * g


def _sigmoid(x):
    return 1.0 / (1.0 + jnp.exp(-x))


def _rope(x, cos, sin):
    return x * cos + pltpu.roll(x, 64, axis=1) * sin


def _adaln_kernel(c_ref, w_ref, b_ref, o_ref):
    c = c_ref[...]
    s = c * _sigmoid(c)
    o_ref[0] = jnp.dot(s, w_ref[0], precision=HIGHEST, preferred_element_type=F32) + b_ref[0]


def _adaln(cond, ada_w, ada_b):
    depth, d, nd = ada_w.shape
    rows = cond.shape[0]
    return pl.pallas_call(
        _adaln_kernel,
        out_shape=jax.ShapeDtypeStruct((depth, rows, nd), F32),
        grid=(depth, nd // d),
        in_specs=[pl.BlockSpec((rows, d), lambda l, n: (0, 0)),
                  pl.BlockSpec((1, d, d), lambda l, n: (l, 0, n)),
                  pl.BlockSpec((1, 1, d), lambda l, n: (l, 0, n))],
        out_specs=pl.BlockSpec((1, rows, d), lambda l, n: (l, 0, n)),
        compiler_params=_params("arbitrary", "arbitrary"),
        name="adaln",
    )(cond, ada_w, ada_b.reshape(depth, 1, nd))


def _ffn_tail(x2, mod, nf_ref, rw_ref, rb_ref, hf_o, idx_o, p_o):
    d = x2.shape[-1]
    hf = _rms(x2, nf_ref[...], d) * (1.0 + mod[4:5]) + mod[3:4]
    hf_o[0] = hf
    logits = jnp.dot(hf, rw_ref[...], precision=HIGHEST, preferred_element_type=F32) + rb_ref[...]
    lane = lax.broadcasted_iota(jnp.int32, logits.shape, 1).astype(F32)
    vals, idxs = [], []
    l = logits
    for _ in range(TOP_K):
        m = jnp.max(l, axis=-1, keepdims=True)
        ik = jnp.min(jnp.where(l == m, lane, float(LANES)), axis=-1, keepdims=True)
        vals.append(m)
        idxs.append(ik)
        l = jnp.where(lane == ik, -jnp.inf, l)
    es = [jnp.exp(v - vals[0]) for v in vals]
    inv = 1.0 / (es[0] + es[1] + es[2] + es[3])
    idx_acc = jnp.zeros_like(logits)
    p_acc = jnp.zeros_like(logits)
    for k in range(TOP_K):
        idx_acc = jnp.where(lane == float(k), idxs[k], idx_acc)
        p_acc = jnp.where(lane == float(k), es[k] * inv, p_acc)
    idx_o[0] = idx_acc.astype(jnp.int32)
    p_o[0] = p_acc


def _attn_proj_kernel(x_ref, mod_ref, nm_ref, win_ref, qn_ref, wuq_ref, kvn_ref, wukv_ref,
                      qng_ref, qrg_ref, kng_ref, krg_ref, gqg_ref, gkg_ref,
                      cm_ref, sm_ref, cg_ref, sg_ref,
                      qm_o, km_o, vm_o, qg_o, kg_o, vg_o):
    x = x_ref[0]
    mod = mod_ref[0, 0]
    d = x.shape[-1]
    h = _rms(x, nm_ref[...], d) * (1.0 + mod[1:2]) + mod[0:1]
    z = _bdot(h, win_ref)
    o1 = MLA_Q_RANK
    o2 = o1 + MLA_KV_RANK
    o3 = o2 + LANES
    o4 = o3 + GQA_HEADS * GQA_HEAD_DIM
    o5 = o4 + GQA_KV_HEADS * GQA_HEAD_DIM
    q = _bdot(_rms(z[:, :o1], qn_ref[...], MLA_Q_RANK), wuq_ref)
    kv = _bdot(_rms(z[:, o1:o2], kvn_ref[...], MLA_KV_RANK), wukv_ref)
    cm, sm, cg, sg = cm_ref[...], sm_ref[...], cg_ref[...], sg_ref[...]
    kr = _rope(_rms(z[:, o2:o3], krg_ref[...], MLA_ROPE), cm, sm)
    for hh in range(MLA_HEADS):
        b0 = hh * 2 * LANES
        qn = _rms(q[:, b0:b0 + LANES], qng_ref[...], MLA_NOPE)
        qr = _rope(_rms(q[:, b0 + LANES:b0 + 2 * LANES], qrg_ref[...], MLA_ROPE), cm, sm)
        qm_o[0, hh] = (jnp.concatenate([qn, qr], axis=1) * MLA_SCALE).astype(BF16)
        kn = _rms(kv[:, b0:b0 + LANES], kng_ref[...], MLA_NOPE)
        km_o[0, hh] = jnp.concatenate([kn, kr], axis=1).astype(BF16)
        vm_o[0, hh] = kv[:, b0 + LANES:b0 + 2 * LANES].astype(BF16)
    for hh in range(GQA_HEADS):
        g = z[:, o3 + hh * LANES:o3 + (hh + 1) * LANES]
        qg_o[0, hh] = (_rope(_rms(g, gqg_ref[...], GQA_HEAD_DIM), cg, sg) * GQA_SCALE).astype(BF16)
    for hh in range(GQA_KV_HEADS):
        g = z[:, o4 + hh * LANES:o4 + (hh + 1) * LANES]
        kg_o[0, hh] = _rope(_rms(g, gkg_ref[...], GQA_HEAD_DIM), cg, sg).astype(BF16)
        vg_o[0, hh] = z[:, o5 + hh * LANES:o5 + (hh + 1) * LANES].astype(BF16)


def _full(shape):
    nd = len(shape)
    return pl.BlockSpec(shape, lambda b, i: (0,) * nd)


def _mod_spec(d):
    return pl.BlockSpec((1, 1, N_MOD, d), lambda b, i: (b, jnp.minimum(i, 1), 0, 0))


def _row_spec(d):
    return pl.BlockSpec((1, TM, d), lambda b, i: (b, i, 0))


def _attn_proj(x, mod, nm, aw, tabs):
    bsz, tu, d = x.shape
    nt = tu // TM
    head_spec = lambda nh, w: pl.BlockSpec((1, nh, TM, w), lambda b, i: (b, 0, i, 0))
    tab_spec = pl.BlockSpec((TM, LANES), lambda b, i: (i, 0))
    weights = [nm, aw["w_in"], aw["q_norm"], aw["w_uq"], aw["kv_norm"], aw["w_ukv"],
               aw["qn_g"], aw["qr_g"], aw["kn_g"], aw["kr_g"], aw["gq_g"], aw["gk_g"]]
    out_shape = [
        jax.ShapeDtypeStruct((bsz, MLA_HEADS, tu, 2 * LANES), BF16),
        jax.ShapeDtypeStruct((bsz, MLA_HEADS, tu, 2 * LANES), BF16),
        jax.ShapeDtypeStruct((bsz, MLA_HEADS, tu, LANES), BF16),
        jax.ShapeDtypeStruct((bsz, GQA_HEADS, tu, LANES), BF16),
        jax.ShapeDtypeStruct((bsz, GQA_KV_HEADS, tu, LANES), BF16),
        jax.ShapeDtypeStruct((bsz, GQA_KV_HEADS, tu, LANES), BF16),
    ]
    return pl.pallas_call(
        _attn_proj_kernel,
        out_shape=out_shape,
        grid=(bsz, nt),
        in_specs=[_row_spec(d), _mod_spec(d)] + [_full(w.shape) for w in weights] + [tab_spec] * 4,
        out_specs=[head_spec(MLA_HEADS, 2 * LANES), head_spec(MLA_HEADS, 2 * LANES),
                   head_spec(MLA_HEADS, LANES), head_spec(GQA_HEADS, LANES),
                   head_spec(GQA_KV_HEADS, LANES), head_spec(GQA_KV_HEADS, LANES)],
        compiler_params=_params("parallel", "parallel"),
        name="attn_proj",
    )(x, mod, *weights, *tabs)


def _attn_kernel(n_ctx, q_ref, k_ref, v_ref, o_ref):
    q = q_ref[0, 0]

    def attend(k, v):
        s = lax.dot_general(q, k, (((1,), (1,)), ((), ())), preferred_element_type=F32)
        m = jnp.max(s, axis=-1, keepdims=True)
        p = jnp.exp(s - m)
        l = jnp.sum(p, axis=-1, keepdims=True)
        o = _dot(p.astype(BF16), v)
        o_ref[0] = (o * (1.0 / l)).astype(o_ref.dtype)

    is_ctx = pl.program_id(2) == 0

    @pl.when(is_ctx)
    def _():
        attend(k_ref[0, 0, :n_ctx], v_ref[0, 0, :n_ctx])

    @pl.when(jnp.logical_not(is_ctx))
    def _():
        attend(k_ref[0, 0], v_ref[0, 0])


def _attention(q, k, v, n_ctx):
    bsz, hq, tu, dk = q.shape
    hk = k.shape[1]
    grp = hq // hk
    dv = v.shape[-1]
    return pl.pallas_call(
        functools.partial(_attn_kernel, n_ctx),
        out_shape=jax.ShapeDtypeStruct((bsz, tu, hq * dv), BF16),
        grid=(bsz, hq, tu // TM),
        in_specs=[pl.BlockSpec((1, 1, TM, dk), lambda b, h, i: (b, h, i, 0)),
                  pl.BlockSpec((1, 1, tu, dk), lambda b, h, i: (b, h // grp, 0, 0)),
                  pl.BlockSpec((1, 1, tu, dv), lambda b, h, i: (b, h // grp, 0, 0))],
        out_specs=pl.BlockSpec((1, TM, dv), lambda b, h, i: (b, i, h)),
        compiler_params=_params("parallel", "parallel", "arbitrary"),
        name="attention",
    )(q, k, v)


def _attn_out_kernel(om_ref, og_ref, x_ref, mod_ref, wo_ref, nf_ref, rw_ref, rb_ref,
                     x_o, hf_o, idx_o, p_o):
    mod = mod_ref[0, 0]
    nm = om_ref.shape[-1]
    o = _dot(om_ref[0], wo_ref[:nm]) + _dot(og_ref[0], wo_ref[nm:])
    x2 = x_ref[0] + mod[2:3] * o
    x_o[0] = x2
    _ffn_tail(x2, mod, nf_ref, rw_ref, rb_ref, hf_o, idx_o, p_o)


def _tail_out_shapes(bsz, tu, d):
    return [jax.ShapeDtypeStruct((bsz, tu, d), F32), jax.ShapeDtypeStruct((bsz, tu, d), F32),
            jax.ShapeDtypeStruct((bsz, tu, LANES), jnp.int32), jax.ShapeDtypeStruct((bsz, tu, LANES), F32)]


def _tail_out_specs(d):
    return [_row_spec(d), _row_spec(d), _row_spec(LANES), _row_spec(LANES)]


def _attn_out(om, og, x, mod, wo, nf, rw, rb):
    bsz, tu, d = x.shape
    return pl.pallas_call(
        _attn_out_kernel,
        out_shape=_tail_out_shapes(bsz, tu, d),
        grid=(bsz, tu // TM),
        in_specs=[_row_spec(om.shape[-1]), _row_spec(og.shape[-1]), _row_spec(d), _mod_spec(d),
                  _full(wo.shape), _full(nf.shape), _full(rw.shape), _full(rb.shape)],
        out_specs=_tail_out_specs(d),
        compiler_params=_params("parallel", "parallel"),
        name="attn_out",
    )(om, og, x, mod, wo, nf, rw, rb)


def _head_sum(x, hs1_ref, hs2_ref):
    hi = x.astype(BF16)
    lo = (x - hi.astype(F32)).astype(BF16)
    s = _dot(hi, hs1_ref[...]) + _dot(lo, hs1_ref[...])
    shi = s.astype(BF16)
    slo = (s - shi.astype(F32)).astype(BF16)
    return _dot(shi, hs2_ref[...]) + _dot(slo, hs2_ref[...])


def _rwkv_feat_kernel(has_vres, nt, x_ref, xp_ref, xn_ref, mod_ref, nm_ref, mix_ref,
                      wr_ref, wk_ref, wv_ref, w0_ref, w1_ref, w2_ref, a0_ref, a1_ref, a2_ref,
                      g1_ref, g2_ref, kk_ref, ka_ref, rk_ref, hs1_ref, hs2_ref, *rest):
    if has_vres:
        v0_ref, v1_ref, v2_ref, vf_ref = rest[:4]
        rest = rest[4:]
    r_o, v_o, kk_o, w_o, k_o, b_o, g_o, bv_o = rest
    i = pl.program_id(1)
    mod = mod_ref[0, 0]
    d = x_ref.shape[-1]
    nm = nm_ref[...]

    def hmod(xx):
        return _rms(xx, nm, d) * (1.0 + mod[1:2]) + mod[0:1]

    h = hmod(x_ref[0])
    has_prev = (i >= 2).astype(F32)
    has_next = jnp.logical_and(i >= 1, i < nt - 1).astype(F32)
    prev_row = hmod(xp_ref[0])[SUBLANES - 1:SUBLANES] * has_prev
    next_row = hmod(xn_ref[0])[0:1] * has_next
    row = lax.broadcasted_iota(jnp.int32, h.shape, 0)
    hp = jnp.where(row == 0, prev_row, pltpu.roll(h, 1, axis=0))
    hn = jnp.where(row == TM - 1, next_row, pltpu.roll(h, TM - 1, axis=0))
    xx = 0.5 * (hp + hn) - h
    xr, xw, xk, xv, xa, xg = (h + xx * mix_ref[m:m + 1] for m in range(N_MIX))

    r = _bdot(xr, wr_ref)
    k = _bdot(xk, wk_ref)
    v = _bdot(xv, wv_ref)
    if has_vres:
        gate = _sigmoid(v0_ref[...] + _bdot(_bdot(xv, v1_ref), v2_ref))
        v = v + (vf_ref[0] - v) * gate
    kk = k * kk_ref[...]
    kk = kk * lax.rsqrt(jnp.maximum(_head_sum(kk * kk, hs1_ref, hs2_ref), 1e-24))
    tw = jnp.tanh(_bdot(xw, w1_ref)).astype(BF16)
    ta = _bdot(xa, a1_ref).astype(BF16)
    ksum = jnp.zeros_like(k)
    for dd in range(2):
        z = w0_ref[dd:dd + 1] + _dot(tw, w2_ref[dd])
        w_o[dd, 0] = jnp.exp(-DECAY_SCALE * _sigmoid(z))
        a = _sigmoid(a0_ref[dd:dd + 1] + _dot(ta, a2_ref[dd]))
        kd = k * (1.0 + (a - 1.0) * ka_ref[...])
        k_o[dd, 0] = kd
        b_o[dd, 0] = kk * a
        ksum = ksum + kd
    g_o[0] = _bdot(_sigmoid(_bdot(xg, g1_ref)), g2_ref)
    bv_o[0] = _head_sum(r * ksum * rk_ref[...], hs1_ref, hs2_ref) * v
    r_o[0] = r
    v_o[0] = v
    kk_o[0] = kk


def _rwkv_feat(x, mod, nm, rp, v_first):
    bsz, tu, d = x.shape
    nt = tu // TM
    has_vres = v_first is not None
    nblk8 = tu // SUBLANES
    per8 = TM // SUBLANES
    prev_spec = pl.BlockSpec((1, SUBLANES, d), lambda b, i: (b, jnp.maximum(i * per8 - 1, 0), 0))
    next_spec = pl.BlockSpec((1, SUBLANES, d), lambda b, i: (b, jnp.minimum((i + 1) * per8, nblk8 - 1), 0))
    weights = [nm, rp["mix"], rp["w_r"], rp["w_k"], rp["w_v"], rp["w0"], rp["w1"], rp["w2"],
               rp["a0"], rp["a1"], rp["a2"], rp["g1"], rp["g2"], rp["k_k"], rp["k_a"], rp["r_k"],
               rp["hs1"], rp["hs2"]]
    extra, extra_specs = [], []
    if has_vres:
        extra = [rp["v0"], rp["v1"], rp["v2"], v_first]
        extra_specs = [_full(rp["v0"].shape), _full(rp["v1"].shape), _full(rp["v2"].shape), _row_spec(d)]
    one = jax.ShapeDtypeStruct((bsz, tu, d), F32)
    two = jax.ShapeDtypeStruct((2, bsz, tu, d), F32)
    dir_spec = pl.BlockSpec((2, 1, TM, d), lambda b, i: (0, b, i, 0))
    return pl.pallas_call(
        functools.partial(_rwkv_feat_kernel, has_vres, nt),
        out_shape=[one, one, one, two, two, two, one, one],
        grid=(bsz, nt),
        in_specs=[_row_spec(d), prev_spec, next_spec, _mod_spec(d)] + [_full(w.shape) for w in weights] + extra_specs,
        out_specs=[_row_spec(d)] * 3 + [dir_spec] * 3 + [_row_spec(d)] * 2,
        compiler_params=_params("parallel", "parallel"),
        name="rwkv_feat",
    )(x, x, x, mod, *weights, *extra)


def _wkv_kernel(r_ref, v_ref, kk_ref, w_ref, k_ref, b_ref, y_ref, s_scr, wr_scr, br_scr, kr_scr):
    dirn = pl.program_id(0)
    j = pl.program_id(1)
    tt, n, lanes = r_ref.shape
    grp = n // SUBLANES

    @pl.when(j == 0)
    def _():
        s_scr[...] = jnp.zeros_like(s_scr)

    r_all = r_ref[...]
    wr_scr[...] = w_ref[0] * r_all
    br_scr[...] = jnp.sum(b_ref[0] * r_all, axis=1)
    kr_scr[...] = jnp.sum(k_ref[0] * r_all, axis=1)

    def bc(ref, t, kx):
        return jnp.broadcast_to(ref[t, pl.ds(kx, 1), :].reshape(1, 1, lanes), (grp, SUBLANES, lanes))

    def bc0(ref, t, kx):
        return jnp.broadcast_to(ref[0, t, pl.ds(kx, 1), :].reshape(1, 1, lanes), (grp, SUBLANES, lanes))

    def step(it, carry):
        t = jnp.where(dirn == 0, it, tt - 1 - it)
        sa = jnp.zeros((grp, SUBLANES, lanes), F32)
        yp = jnp.zeros((grp, SUBLANES, lanes), F32)
        for kx in range(n):
            sk = s_scr[kx].reshape(grp, SUBLANES, lanes)
            sa = sa - sk * bc(kk_ref, t, kx)
            yp = yp + sk * bc(wr_scr, t, kx)
        vt = v_ref[t].reshape(grp, SUBLANES, lanes)
        for kx in range(n):
            sk = s_scr[kx].reshape(grp, SUBLANES, lanes)
            new = sk * bc0(w_ref, t, kx) + sa * bc0(b_ref, t, kx) + vt * bc0(k_ref, t, kx)
            s_scr[kx] = new.reshape(n, lanes)
        br = br_scr[pl.ds(t, 1), :].reshape(1, 1, lanes)
        kr = kr_scr[pl.ds(t, 1), :].reshape(1, 1, lanes)
        y = yp + sa * br + vt * kr
        y_ref[0, t] = y.reshape(n, lanes)
        return carry

    lax.fori_loop(0, tt, step, 0)


def _wkv_scan(r, v, kk, w, k, b, n_ctx):
    tu, n, lanes = r.shape
    nb = tu // TT_SCAN
    cb = n_ctx // TT_SCAN

    def blk(dd, j):
        back = jnp.where(j < cb, cb - 1 - j, nb - 1 + cb - j)
        return jnp.where(dd == 0, j, back)

    shared = pl.BlockSpec((TT_SCAN, n, lanes), lambda dd, j: (blk(dd, j), 0, 0))
    per_dir = pl.BlockSpec((1, TT_SCAN, n, lanes), lambda dd, j: (dd, blk(dd, j), 0, 0))
    return pl.pallas_call(
        _wkv_kernel,
        out_shape=jax.ShapeDtypeStruct((2, tu, n, lanes), F32),
        grid=(2, nb),
        in_specs=[shared, shared, shared, per_dir, per_dir, per_dir],
        out_specs=per_dir,
        scratch_shapes=[pltpu.VMEM((n, n, lanes), F32), pltpu.VMEM((TT_SCAN, n, lanes), F32),
                        pltpu.VMEM((TT_SCAN, lanes), F32), pltpu.VMEM((TT_SCAN, lanes), F32)],
        compiler_params=_params("arbitrary", "arbitrary"),
        name="wkv_scan",
    )(r, v, kk, w, k, b)


def _rwkv_out_kernel(y_ref, g_ref, bv_ref, x_ref, mod_ref, lnw_ref, lnb_ref, wo_ref, hs1_ref, hs2_ref,
                     nf_ref, rw_ref, rb_ref, x_o, hf_o, idx_o, p_o):
    mod = mod_ref[0, 0]
    y = y_ref[0]
    inv_n = 1.0 / RWKV_HEAD
    mu = _head_sum(y, hs1_ref, hs2_ref) * inv_n
    dlt = y - mu
    var = _head_sum(dlt * dlt, hs1_ref, hs2_ref) * inv_n
    yn = dlt * lax.rsqrt(var + RWKV_GN_EPS) * lnw_ref[...] + lnb_ref[...]
    o = _bdot((yn + bv_ref[0]) * g_ref[0], wo_ref)
    x2 = x_ref[0] + mod[2:3] * o
    x_o[0] = x2
    _ffn_tail(x2, mod, nf_ref, rw_ref, rb_ref, hf_o, idx_o, p_o)


def _rwkv_out(y, g, bv, x, mod, rp, nf, rw, rb):
    bsz, tu, d = x.shape
    weights = [rp["ln_w"], rp["ln_b"], rp["w_o"], rp["hs1"], rp["hs2"], nf, rw, rb]
    return pl.pallas_call(
        _rwkv_out_kernel,
        out_shape=_tail_out_shapes(bsz, tu, d),
        grid=(bsz, tu // TM),
        in_specs=[_row_spec(d)] * 4 + [_mod_spec(d)] + [_full(w.shape) for w in weights],
        out_specs=_tail_out_specs(d),
        compiler_params=_params("parallel", "parallel"),
        name="rwkv_out",
    )(y, g, bv, x, mod, *weights)


def _dispatch_kernel(pos_ref, h_ref, init_hbm, out_hbm, hbuf, sem):
    del init_hbm
    h = h_ref[...]
    for s in range(SUBLANES):
        hbuf[:, s, :] = h[:, s * LANES:(s + 1) * LANES]

    def issue(jj, carry):
        for kk in range(TOP_K):
            pltpu.make_async_copy(hbuf.at[jj], out_hbm.at[pos_ref[jj * TOP_K + kk]], sem).start()
        return carry

    lax.fori_loop(0, TM, issue, 0)
    for _ in range(TOP_K):
        pltpu.make_async_copy(hbuf, out_hbm.at[pl.ds(0, TM)], sem).wait()


def _dispatch(pos_flat, h2, n_rows):
    n_tok, d = h2.shape
    init = jnp.zeros((n_rows, SUBLANES, LANES), h2.dtype)
    return pl.pallas_call(
        _dispatch_kernel,
        out_shape=jax.ShapeDtypeStruct(init.shape, init.dtype),
        grid=(n_tok // TM,),
        in_specs=[pl.BlockSpec((TM * TOP_K,), lambda i: (i,), memory_space=pltpu.SMEM),
                  pl.BlockSpec((TM, d), lambda i: (i, 0)), pl.BlockSpec(memory_space=pl.ANY)],
        out_specs=pl.BlockSpec(memory_space=pl.ANY),
        scratch_shapes=[pltpu.VMEM((TM, SUBLANES, LANES), h2.dtype), pltpu.SemaphoreType.DMA(())],
        input_output_aliases={2: 0},
        compiler_params=_params("arbitrary"),
        name="moe_dispatch",
    )(pos_flat, h2, init)


def _deinterleave_kernel(w_ref, p_ref, o_ref):
    o_ref[...] = _dot(w_ref[...].astype(BF16), p_ref[...]).astype(BF16)


def _deinterleave_w1(w1):
    f2 = w1.shape[-1]
    rows = int(np.prod(w1.shape[:-1]))
    col = jnp.arange(f2, dtype=jnp.int32)
    src = jnp.where(col < f2 // 2, 2 * col, 2 * (col - f2 // 2) + 1)
    perm = (col[:, None] == src[None, :]).astype(BF16)
    out = pl.pallas_call(
        _deinterleave_kernel,
        out_shape=jax.ShapeDtypeStruct((rows, f2), BF16),
        grid=(rows // TM_FFN,),
        in_specs=[pl.BlockSpec((TM_FFN, f2), lambda i: (i, 0)), pl.BlockSpec((f2, f2), lambda i: (0, 0))],
        out_specs=pl.BlockSpec((TM_FFN, f2), lambda i: (i, 0)),
        compiler_params=_params("parallel"),
        name="w1_deinterleave",
    )(w1.reshape(rows, f2), perm)
    return out.reshape(w1.shape)


def _rows_from_tiles(ref):
    return jnp.concatenate([ref[:, s, :] for s in range(SUBLANES)], axis=1)


def _moe_ffn_kernel(te_ref, nu_ref, hs_ref, w1_ref, b1_ref, w2_ref, b2_ref, ys_ref):
    t = pl.program_id(0)

    @pl.when(t < nu_ref[0])
    def _():
        x = _rows_from_tiles(hs_ref).astype(BF16)
        u = _dot(x, w1_ref[0]) + b1_ref[0]
        f = u.shape[-1] // 2
        glu = jnp.minimum(u[:, :f], SWIGLU_LIMIT)
        lin = jnp.clip(u[:, f:], -SWIGLU_LIMIT, SWIGLU_LIMIT)
        act = glu * _sigmoid(SWIGLU_ALPHA * glu) * (lin + 1.0)
        y = _dot(act.astype(BF16), w2_ref[0]) + b2_ref[0]
        for s in range(SUBLANES):
            ys_ref[:, s, :] = y[:, s * LANES:(s + 1) * LANES]

    @pl.when(t >= nu_ref[0])
    def _():
        ys_ref[...] = jnp.zeros_like(ys_ref)


def _moe_ffn(tile_expert, n_used, hs3, w1, b1, w2, b2):
    n_rows = hs3.shape[0]
    n_tiles = n_rows // TM_FFN
    d = w1.shape[1]
    f2 = w1.shape[2]
    row_map = lambda t, te, nu: (jnp.minimum(t, nu[0] - 1), 0, 0)
    exp_map = lambda t, te, nu: (te[t], 0, 0)
    grid_spec = pltpu.PrefetchScalarGridSpec(
        num_scalar_prefetch=2,
        grid=(n_tiles,),
        in_specs=[pl.BlockSpec((TM_FFN, SUBLANES, LANES), row_map),
                  pl.BlockSpec((1, d, f2), exp_map), pl.BlockSpec((1, 1, f2), exp_map),
                  pl.BlockSpec((1, f2 // 2, d), exp_map), pl.BlockSpec((1, 1, d), exp_map)],
        out_specs=pl.BlockSpec((TM_FFN, SUBLANES, LANES), lambda t, te, nu: (t, 0, 0)),
    )
    return pl.pallas_call(
        _moe_ffn_kernel,
        out_shape=jax.ShapeDtypeStruct(hs3.shape, F32),
        grid_spec=grid_spec,
        compiler_params=_params("arbitrary"),
        name="moe_ffn",
    )(tile_expert, n_used, hs3, w1, b1, w2, b2)


def _combine_kernel(pos_ref, ys_hbm, x_ref, p_ref, mod_ref, x_o, ybuf, sem):
    def issue(jj, carry):
        for kk in range(TOP_K):
            pltpu.make_async_copy(ys_hbm.at[pos_ref[jj * TOP_K + kk]], ybuf.at[kk * TM + jj], sem).start()
        return carry

    lax.fori_loop(0, TM, issue, 0)
    n = TM * TOP_K
    pltpu.make_async_copy(ys_hbm.at[pl.ds(0, n)], ybuf, sem).wait()
    p = p_ref[0]
    acc = jnp.zeros(x_ref.shape[1:], F32)
    for kk in range(TOP_K):
        yk = jnp.concatenate([ybuf[pl.ds(kk * TM, TM), s, :] for s in range(SUBLANES)], axis=1)
        acc = acc + p[:, kk:kk + 1] * yk
    x_o[0] = x_ref[0] + mod_ref[0, 0][5:6] * acc


def _combine(pos_flat, ys3, x, p_pad, mod):
    bsz, tu, d = x.shape
    nt = tu // TM
    return pl.pallas_call(
        _combine_kernel,
        out_shape=jax.ShapeDtypeStruct(x.shape, F32),
        grid=(bsz, nt),
        in_specs=[pl.BlockSpec((TM * TOP_K,), lambda b, i: (b * nt + i,), memory_space=pltpu.SMEM),
                  pl.BlockSpec(memory_space=pl.ANY), _row_spec(d), _row_spec(LANES), _mod_spec(d)],
        out_specs=_row_spec(d),
        scratch_shapes=[pltpu.VMEM((TM * TOP_K, SUBLANES, LANES), F32), pltpu.SemaphoreType.DMA(())],
        compiler_params=_params("arbitrary", "arbitrary"),
        name="moe_combine",
    )(pos_flat, ys3, x, p_pad, mod)


def _moe(x, hf, idx_pad, p_pad, mod, w1, b1, w2, b2):
    bsz, tu, d = x.shape
    n_tok = bsz * tu
    n_rows = n_tok * TOP_K + N_EXPERTS * TM_FFN
    n_tiles = n_rows // TM_FFN
    idx = idx_pad.reshape(n_tok, LANES)[:, :TOP_K]
    onehot = jnp.sum((idx[:, :, None] == jnp.arange(N_EXPERTS, dtype=jnp.int32)).astype(jnp.int32), axis=1)
    csum = jnp.cumsum(onehot, axis=0)
    counts = csum[-1]
    padded = ((counts + TM_FFN - 1) // TM_FFN) * TM_FFN
    ends = jnp.cumsum(padded)
    pos = (ends - padded)[idx] + jnp.take_along_axis(csum - onehot, idx, axis=1)
    n_used = (ends[-1] // TM_FFN).astype(jnp.int32)
    tile_start = jnp.arange(n_tiles, dtype=jnp.int32) * TM_FFN
    tile_expert = jnp.sum((tile_start[:, None] >= ends[None, :]).astype(jnp.int32), axis=1)
    last = jnp.take(tile_expert, jnp.maximum(n_used - 1, 0))
    tile_expert = jnp.where(jnp.arange(n_tiles) < n_used, tile_expert, last).astype(jnp.int32)
    pos_flat = pos.reshape(-1).astype(jnp.int32)

    hs3 = _dispatch(pos_flat, hf.reshape(n_tok, d), n_rows)
    ys3 = _moe_ffn(tile_expert, n_used.reshape(1), hs3, w1, b1, w2, b2)
    return _combine(pos_flat, ys3, x, p_pad, mod)


def _rope_tables(n_ctx, n_lat):
    def angles(rot_dim):
        rows = n_lat // GRID_W
        row = jnp.repeat(jnp.arange(rows, dtype=F32), GRID_W)
        col = jnp.tile(jnp.arange(GRID_W, dtype=F32), rows)
        half = rot_dim // 2
        inv_freq = ROPE_THETA ** (-jnp.arange(0, half, 2, dtype=F32) / half)
        return row[:, None] * inv_freq[None, :], col[:, None] * inv_freq[None, :]

    def with_ctx(t, fill):
        return jnp.concatenate([jnp.full((n_ctx, LANES), fill, F32), t], axis=0)

    ar, ac = angles(MLA_ROPE)
    one = jnp.ones((n_lat, 32), F32)
    zero = jnp.zeros((n_lat, 32), F32)
    cm = jnp.concatenate([jnp.cos(ar), jnp.cos(ac), one, jnp.cos(ar), jnp.cos(ac), one], axis=1)
    sm = jnp.concatenate([-jnp.sin(ar), -jnp.sin(ac), zero, jnp.sin(ar), jnp.sin(ac), zero], axis=1)
    ar, ac = angles(GQA_HEAD_DIM)
    cg = jnp.concatenate([jnp.cos(ar), jnp.cos(ac), jnp.cos(ar), jnp.cos(ac)], axis=1)
    sg = jnp.concatenate([-jnp.sin(ar), -jnp.sin(ac), jnp.sin(ar), jnp.sin(ac)], axis=1)
    return with_ctx(cm, 1.0), with_ctx(sm, 0.0), with_ctx(cg, 1.0), with_ctx(sg, 0.0)


_PERM_GQA = np.concatenate([np.arange(0, 32), np.arange(64, 96), np.arange(32, 64), np.arange(96, 128)])
_SRC_MLA = np.zeros(LANES, np.int32)
_MASK_MLA = np.zeros(LANES, np.float32)
_SRC_MLA[0:16], _SRC_MLA[16:32], _SRC_MLA[64:80], _SRC_MLA[80:96] = (
    np.arange(0, 16), np.arange(32, 48), np.arange(16, 32), np.arange(48, 64))
_MASK_MLA[0:32] = 1.0
_MASK_MLA[64:96] = 1.0


def _pad_rope64(w):
    return jnp.take(w, _SRC_MLA, axis=-1) * _MASK_MLA


def _attn_weights(e, w_in, q_norm, w_uq, kv_norm, w_ukv, qn_g, qr_g, kn_g, kr_g, gq_g, gk_g, w_out):
    s1 = MLA_Q_RANK
    s2 = s1 + MLA_KV_RANK
    s3 = s2 + MLA_ROPE
    s4 = s3 + GQA_HEADS * GQA_HEAD_DIM
    s5 = s4 + GQA_KV_HEADS * GQA_HEAD_DIM
    wi = w_in[e]
    d = wi.shape[0]
    gq = wi[:, s3:s4].reshape(d, GQA_HEADS, GQA_HEAD_DIM)[:, :, _PERM_GQA].reshape(d, -1)
    gk = wi[:, s4:s5].reshape(d, GQA_KV_HEADS, GQA_HEAD_DIM)[:, :, _PERM_GQA].reshape(d, -1)
    w_in_p = jnp.concatenate([wi[:, :s2], _pad_rope64(wi[:, s2:s3]), gq, gk, wi[:, s5:]], axis=1)
    uq = w_uq[e].reshape(MLA_Q_RANK, MLA_HEADS, MLA_NOPE + MLA_ROPE)
    uq_p = jnp.concatenate([uq[:, :, :MLA_NOPE], _pad_rope64(uq[:, :, MLA_NOPE:])], axis=-1)
    row = lambda g: g.reshape(1, -1).astype(F32)
    return dict(
        w_in=w_in_p.astype(BF16), q_norm=row(q_norm[e]), w_uq=uq_p.reshape(MLA_Q_RANK, -1).astype(BF16),
        kv_norm=row(kv_norm[e]), w_ukv=w_ukv[e].astype(BF16),
        qn_g=row(qn_g[e]), qr_g=row(_pad_rope64(qr_g[e])), kn_g=row(kn_g[e]), kr_g=row(_pad_rope64(kr_g[e])),
        gq_g=row(gq_g[e][_PERM_GQA]), gk_g=row(gk_g[e][_PERM_GQA]), w_out=w_out[e].astype(BF16))


def _pad_cols(w, n):
    return jnp.pad(w, ((0, 0), (0, n - w.shape[1])))


def _pad_rows(w, n):
    return jnp.pad(w, ((0, n - w.shape[0]), (0, 0)))


def _lora_pair(w1, w2):
    rank = w1.shape[-1]
    down = jnp.concatenate([w1[0], w1[1]], axis=1)
    up = jnp.stack([jnp.pad(w2[0], ((0, rank), (0, 0))), jnp.pad(w2[1], ((rank, 0), (0, 0)))])
    return down.astype(BF16), up.astype(BF16)


def _rwkv_weights(j, mix, w_r, w_k, w_v, w_o, w0, w1, w2, a0, a1, a2, v0, v1, v2, g1, g2, k_k, k_a, r_k, ln_w, ln_b):
    d = w_r.shape[1]
    row = lambda g: g.reshape(1, -1).astype(F32)
    heads = np.arange(d) // RWKV_HEAD
    hs1 = (heads[:, None] == np.arange(LANES)[None, :]).astype(np.float32)
    w1c, w2p = _lora_pair(w1[j], w2[j])
    a1c, a2p = _lora_pair(a1[j], a2[j])
    rp = dict(mix=mix[j], w_r=w_r[j].astype(BF16), w_k=w_k[j].astype(BF16), w_v=w_v[j].astype(BF16),
              w_o=w_o[j].astype(BF16), w0=w0[j], w1=w1c, w2=w2p, a0=a0[j], a1=a1c, a2=a2p,
              g1=g1[j].astype(BF16), g2=g2[j].astype(BF16), k_k=row(k_k[j]), k_a=row(k_a[j]), r_k=row(r_k[j]),
              ln_w=row(ln_w[j]), ln_b=row(ln_b[j]),
              hs1=jnp.asarray(hs1, BF16), hs2=jnp.asarray(hs1.T, BF16))
    if j > 0:
        rp.update(v0=row(v0[j - 1]), v1=_pad_cols(v1[j - 1], LANES).astype(BF16),
                  v2=_pad_rows(v2[j - 1], LANES).astype(BF16))
    return rp


def _to_scan(t):
    lead = t.shape[:-3]
    bsz, tu, d = t.shape[-3:]
    nl = len(lead)
    t = t.reshape(*lead, bsz, tu, d // RWKV_HEAD, RWKV_HEAD)
    perm = tuple(range(nl)) + (nl + 1, nl + 3, nl + 0, nl + 2)
    return t.transpose(perm).reshape(*lead, tu, RWKV_HEAD, bsz * (d // RWKV_HEAD))


def kernel(x, c, ctx, c_ctx, ada_w, ada_b, norm_mix, norm_ffn, attn_w_in, mla_q_norm, mla_w_uq, mla_kv_norm, mla_w_ukv, mla_qn_g, mla_qr_g, mla_kn_g, mla_kr_g, gqa_q_g, gqa_k_g, attn_w_out, rwkv_mix, rwkv_w_r, rwkv_w_k, rwkv_w_v, rwkv_w_o, rwkv_w0, rwkv_w1, rwkv_w2, rwkv_a0, rwkv_a1, rwkv_a2, rwkv_v0, rwkv_v1, rwkv_v2, rwkv_g1, rwkv_g2, rwkv_k_k, rwkv_k_a, rwkv_r_k, rwkv_ln_w, rwkv_ln_b, moe_router_w, moe_router_b, moe_w1, moe_b1, moe_w2, moe_b2):
    bsz, n_lat, d = x.shape
    n_ctx = ctx.shape[1]
    depth = ada_w.shape[0]
    tu = n_ctx + n_lat
    assert n_ctx % TM == 0 and n_lat % TM == 0 and n_ctx % TT_SCAN == 0 and d == SUBLANES * LANES
    assert (bsz * tu * TOP_K) % TM_FFN == 0

    rows = -(-(bsz + 1) // SUBLANES) * SUBLANES
    cond = jnp.zeros((rows, d), F32).at[:bsz].set(c).at[bsz].set(c_ctx)
    mods = _adaln(cond, ada_w, ada_b)
    mods_lat = mods[:, :bsz].reshape(depth, bsz, 1, N_MOD, d)
    mods_ctx = jnp.broadcast_to(mods[:, bsz].reshape(depth, 1, 1, N_MOD, d), mods_lat.shape)
    mods = jnp.concatenate([mods_ctx, mods_lat], axis=2)

    tabs = _rope_tables(n_ctx, n_lat)
    xs = jnp.concatenate([ctx, x], axis=1)

    n_e, _, f2 = moe_w1.shape[1:]
    rw_all = jnp.pad(moe_router_w, ((0, 0), (0, 0), (0, LANES - n_e)))
    rb_all = jnp.pad(moe_router_b, ((0, 0), (0, LANES - n_e)), constant_values=-1e30).reshape(depth, 1, LANES)

    w1_all = _deinterleave_w1(moe_w1)

    v_first = None
    for i in range(depth):
        mod = mods[i]
        nm = norm_mix[i].reshape(1, d)
        nf = norm_ffn[i].reshape(1, d)
        rw, rb = rw_all[i], rb_all[i]
        if i % 2 == 0:
            aw = _attn_weights(i // 2, attn_w_in, mla_q_norm, mla_w_uq, mla_kv_norm, mla_w_ukv, mla_qn_g,
                               mla_qr_g, mla_kn_g, mla_kr_g, gqa_q_g, gqa_k_g, attn_w_out)
            qm, km, vm, qg, kg, vg = _attn_proj(xs, mod, nm, aw, tabs)
            om = _attention(qm, km, vm, n_ctx)
            og = _attention(qg, kg, vg, n_ctx)
            xs, hf, idx_pad, p_pad = _attn_out(om, og, xs, mod, aw["w_out"], nf, rw, rb)
        else:
            j = i // 2
            rp = _rwkv_weights(j, rwkv_mix, rwkv_w_r, rwkv_w_k, rwkv_w_v, rwkv_w_o, rwkv_w0, rwkv_w1, rwkv_w2,
                               rwkv_a0, rwkv_a1, rwkv_a2, rwkv_v0, rwkv_v1, rwkv_v2, rwkv_g1, rwkv_g2,
                               rwkv_k_k, rwkv_k_a, rwkv_r_k, rwkv_ln_w, rwkv_ln_b)
            r, v, kk, w, k, b, g, bv = _rwkv_feat(xs, mod, nm, rp, v_first if j > 0 else None)
            if j == 0:
                v_first = v
            y = _wkv_scan(_to_scan(r), _to_scan(v), _to_scan(kk), _to_scan(w), _to_scan(k), _to_scan(b), n_ctx)
            y = (y[0] + y[1]).reshape(tu, RWKV_HEAD, bsz, d // RWKV_HEAD).transpose(2, 0, 3, 1).reshape(bsz, tu, d)
            xs, hf, idx_pad, p_pad = _rwkv_out(y, g, bv, xs, mod, rp, nf, rw, rb)
        w1 = w1_all[i]
        b1 = jnp.concatenate([moe_b1[i][:, 0::2], moe_b1[i][:, 1::2]], axis=-1).reshape(n_e, 1, f2)
        w2 = moe_w2[i].astype(BF16)
        b2 = moe_b2[i].reshape(n_e, 1, d)
        xs = _moe(xs, hf, idx_pad, p_pad, mod, w1, b1, w2, b2)
    return xs[:, n_ctx:]
```

```python
import functools

import jax
import jax.numpy as jnp
import numpy as np
from jax import lax
from jax.experimental import pallas as pl
from jax.experimental.pallas import tpu as pltpu

F32 = jnp.float32
BF16 = jnp.bfloat16
HIGHEST = lax.Precision.HIGHEST

SUBLANES = 8
LANES = 128
VMEM_LIMIT = 56 * 1024 * 1024

GRID_W = 64
ROPE_THETA = 10000.0
NORM_EPS = 1e-6
N_MOD = 6

MLA_HEADS = 4
MLA_Q_RANK = 384
MLA_KV_RANK = 256
MLA_NOPE = 128
MLA_ROPE = 64
MLA_V = 128
MLA_SCALE = (MLA_NOPE + MLA_ROPE) ** -0.5
GQA_HEADS = 4
GQA_KV_HEADS = 2
GQA_HEAD_DIM = 128
GQA_SCALE = GQA_HEAD_DIM ** -0.5

RWKV_HEAD = 64
RWKV_GN_EPS = 64e-5
N_MIX = 6
DECAY_SCALE = float(np.exp(-0.5))

N_EXPERTS = 32
TOP_K = 4
SWIGLU_ALPHA = 1.702
SWIGLU_LIMIT = 7.0

TM = 256
TM_FFN = 512
WKV_CHUNK = 64
WKV_PAIR_GROUP = 8


def _params(*sem):
    return pltpu.CompilerParams(dimension_semantics=sem, vmem_limit_bytes=VMEM_LIMIT)


def _dot(a, b):
    return jnp.dot(a, b, preferred_element_type=F32)


def _bdot(a, b_ref):
    return jnp.dot(a.astype(BF16), b_ref[...], preferred_element_type=F32)


def _rms(x, g, n):
    ms = jnp.sum(x * x, axis=-1, keepdims=True) * (1.0 / n)
    return x * lax.rsqrt(ms + NORM_EPS) * g


def _sigmoid(x):
    return 1.0 / (1.0 + jnp.exp(-x))


def _rope(x, cos, sin):
    return x * cos + pltpu.roll(x, 64, axis=1) * sin


def _adaln_kernel(c_ref, w_ref, b_ref, o_ref):
    c = c_ref[...]
    s = c * _sigmoid(c)
    o_ref[0] = jnp.dot(s, w_ref[0], precision=HIGHEST, preferred_element_type=F32) + b_ref[0]


def _adaln(cond, ada_w, ada_b):
    depth, d, nd = ada_w.shape
    rows = cond.shape[0]
    return pl.pallas_call(
        _adaln_kernel,
        out_shape=jax.ShapeDtypeStruct((depth, rows, nd), F32),
        grid=(depth, nd // d),
        in_specs=[pl.BlockSpec((rows, d), lambda l, n: (0, 0)),
                  pl.BlockSpec((1, d, d), lambda l, n: (l, 0, n)),
                  pl.BlockSpec((1, 1, d), lambda l, n: (l, 0, n))],
        out_specs=pl.BlockSpec((1, rows, d), lambda l, n: (l, 0, n)),
        compiler_params=_params("arbitrary", "arbitrary"),
        name="adaln",
    )(cond, ada_w, ada_b.reshape(depth, 1, nd))


def _ffn_tail(x2, mod, nf_ref, rw_ref, rb_ref, hf_o, idx_o, p_o):
    d = x2.shape[-1]
    hf = _rms(x2, nf_ref[...], d) * (1.0 + mod[4:5]) + mod[3:4]
    hf_o[0] = hf
    logits = jnp.dot(hf, rw_ref[...], precision=HIGHEST, preferred_element_type=F32) + rb_ref[...]
    lane = lax.broadcasted_iota(jnp.int32, logits.shape, 1).astype(F32)
    vals, idxs = [], []
    l = logits
    for _ in range(TOP_K):
        m = jnp.max(l, axis=-1, keepdims=True)
        ik = jnp.min(jnp.where(l == m, lane, float(LANES)), axis=-1, keepdims=True)
        vals.append(m)
        idxs.append(ik)
        l = jnp.where(lane == ik, -jnp.inf, l)
    es = [jnp.exp(v - vals[0]) for v in vals]
    inv = 1.0 / (es[0] + es[1] + es[2] + es[3])
    idx_acc = jnp.zeros_like(logits)
    p_acc = jnp.zeros_like(logits)
    for k in range(TOP_K):
        idx_acc = jnp.where(lane == float(k), idxs[k], idx_acc)
        p_acc = jnp.where(lane == float(k), es[k] * inv, p_acc)
    idx_o[0] = idx_acc.astype(jnp.int32)
    p_o[0] = p_acc


def _attn_proj_kernel(x_ref, mod_ref, nm_ref, win_ref, qn_ref, wuq_ref, kvn_ref, wukv_ref,
                      qng_ref, qrg_ref, kng_ref, krg_ref, gqg_ref, gkg_ref,
                      cm_ref, sm_ref, cg_ref, sg_ref,
                      qm_o, km_o, vm_o, qg_o, kg_o, vg_o):
    x = x_ref[0]
    mod = mod_ref[0, 0]
    d = x.shape[-1]
    h = _rms(x, nm_ref[...], d) * (1.0 + mod[1:2]) + mod[0:1]
    z = _bdot(h, win_ref)
    o1 = MLA_Q_RANK
    o2 = o1 + MLA_KV_RANK
    o3 = o2 + LANES
    o4 = o3 + GQA_HEADS * GQA_HEAD_DIM
    o5 = o4 + GQA_KV_HEADS * GQA_HEAD_DIM
    q = _bdot(_rms(z[:, :o1], qn_ref[...], MLA_Q_RANK), wuq_ref)
    kv = _bdot(_rms(z[:, o1:o2], kvn_ref[...], MLA_KV_RANK), wukv_ref)
    cm, sm, cg, sg = cm_ref[...], sm_ref[...], cg_ref[...], sg_ref[...]
    kr = _rope(_rms(z[:, o2:o3], krg_ref[...], MLA_ROPE), cm, sm)
    for hh in range(MLA_HEADS):
        b0 = hh * 2 * LANES
        qn = _rms(q[:, b0:b0 + LANES], qng_ref[...], MLA_NOPE)
        qr = _rope(_rms(q[:, b0 + LANES:b0 + 2 * LANES], qrg_ref[...], MLA_ROPE), cm, sm)
        qm_o[0, hh] = (jnp.concatenate([qn, qr], axis=1) * MLA_SCALE).astype(BF16)
        kn = _rms(kv[:, b0:b0 + LANES], kng_ref[...], MLA_NOPE)
        km_o[0, hh] = jnp.concatenate([kn, kr], axis=1).astype(BF16)
        vm_o[0, hh] = kv[:, b0 + LANES:b0 + 2 * LANES].astype(BF16)
    for hh in range(GQA_HEADS):
        g = z[:, o3 + hh * LANES:o3 + (hh + 1) * LANES]
        qg_o[0, hh] = (_rope(_rms(g, gqg_ref[...], GQA_HEAD_DIM), cg, sg) * GQA_SCALE).astype(BF16)
    for hh in range(GQA_KV_HEADS):
        g = z[:, o4 + hh * LANES:o4 + (hh + 1) * LANES]
        kg_o[0, hh] = _rope(_rms(g, gkg_ref[...], GQA_HEAD_DIM), cg, sg).astype(BF16)
        vg_o[0, hh] = z[:, o5 + hh * LANES:o5 + (hh + 1) * LANES].astype(BF16)


def _full(shape):
    nd = len(shape)
    return pl.BlockSpec(shape, lambda b, i: (0,) * nd)


def _mod_spec(d):
    return pl.BlockSpec((1, 1, N_MOD, d), lambda b, i: (b, jnp.minimum(i, 1), 0, 0))


def _row_spec(d):
    return pl.BlockSpec((1, TM, d), lambda b, i: (b, i, 0))


def _attn_proj(x, mod, nm, aw, tabs):
    bsz, tu, d = x.shape
    nt = tu // TM
    head_spec = lambda nh, w: pl.BlockSpec((1, nh, TM, w), lambda b, i: (b, 0, i, 0))
    tab_spec = pl.BlockSpec((TM, LANES), lambda b, i: (i, 0))
    weights = [nm, aw["w_in"], aw["q_norm"], aw["w_uq"], aw["kv_norm"], aw["w_ukv"],
               aw["qn_g"], aw["qr_g"], aw["kn_g"], aw["kr_g"], aw["gq_g"], aw["gk_g"]]
    out_shape = [
        jax.ShapeDtypeStruct((bsz, MLA_HEADS, tu, 2 * LANES), BF16),
        jax.ShapeDtypeStruct((bsz, MLA_HEADS, tu, 2 * LANES), BF16),
        jax.ShapeDtypeStruct((bsz, MLA_HEADS, tu, LANES), BF16),
        jax.ShapeDtypeStruct((bsz, GQA_HEADS, tu, LANES), BF16),
        jax.ShapeDtypeStruct((bsz, GQA_KV_HEADS, tu, LANES), BF16),
        jax.ShapeDtypeStruct((bsz, GQA_KV_HEADS, tu, LANES), BF16),
    ]
    return pl.pallas_call(
        _attn_proj_kernel,
        out_shape=out_shape,
        grid=(bsz, nt),
        in_specs=[_row_spec(d), _mod_spec(d)] + [_full(w.shape) for w in weights] + [tab_spec] * 4,
        out_specs=[head_spec(MLA_HEADS, 2 * LANES), head_spec(MLA_HEADS, 2 * LANES),
                   head_spec(MLA_HEADS, LANES), head_spec(GQA_HEADS, LANES),
                   head_spec(GQA_KV_HEADS, LANES), head_spec(GQA_KV_HEADS, LANES)],
        compiler_params=_params("parallel", "parallel"),
        name="attn_proj",
    )(x, mod, *weights, *tabs)


def _attn_kernel(n_ctx, q_ref, k_ref, v_ref, o_ref):
    q = q_ref[0, 0]

    def attend(k, v):
        s = lax.dot_general(q, k, (((1,), (1,)), ((), ())), preferred_element_type=F32)
        m = jnp.max(s, axis=-1, keepdims=True)
        p = jnp.exp(s - m)
        l = jnp.sum(p, axis=-1, keepdims=True)
        o = _dot(p.astype(BF16), v)
        o_ref[0] = (o * (1.0 / l)).astype(o_ref.dtype)

    is_ctx = pl.program_id(2) == 0

    @pl.when(is_ctx)
    def _():
        attend(k_ref[0, 0, :n_ctx], v_ref[0, 0, :n_ctx])

    @pl.when(jnp.logical_not(is_ctx))
    def _():
        attend(k_ref[0, 0], v_ref[0, 0])


def _attention(q, k, v, n_ctx):
    bsz, hq, tu, dk = q.shape
    hk = k.shape[1]
    grp = hq // hk
    dv = v.shape[-1]
    return pl.pallas_call(
        functools.partial(_attn_kernel, n_ctx),
        out_shape=jax.ShapeDtypeStruct((bsz, tu, hq * dv), BF16),
        grid=(bsz, hq, tu // TM),
        in_specs=[pl.BlockSpec((1, 1, TM, dk), lambda b, h, i: (b, h, i, 0)),
                  pl.BlockSpec((1, 1, tu, dk), lambda b, h, i: (b, h // grp, 0, 0)),
                  pl.BlockSpec((1, 1, tu, dv), lambda b, h, i: (b, h // grp, 0, 0))],
        out_specs=pl.BlockSpec((1, TM, dv), lambda b, h, i: (b, i, h)),
        compiler_params=_params("parallel", "parallel", "arbitrary"),
        name="attention",
    )(q, k, v)


def _attn_out_kernel(om_ref, og_ref, x_ref, mod_ref, wo_ref, nf_ref, rw_ref, rb_ref,
                     x_o, hf_o, idx_o, p_o):
    mod = mod_ref[0, 0]
    nm = om_ref.shape[-1]
    o = _dot(om_ref[0], wo_ref[:nm]) + _dot(og_ref[0], wo_ref[nm:])
    x2 = x_ref[0] + mod[2:3] * o
    x_o[0] = x2
    _ffn_tail(x2, mod, nf_ref, rw_ref, rb_ref, hf_o, idx_o, p_o)


def _tail_out_shapes(bsz, tu, d):
    return [jax.ShapeDtypeStruct((bsz, tu, d), F32), jax.ShapeDtypeStruct((bsz, tu, d), F32),
            jax.ShapeDtypeStruct((bsz, tu, LANES), jnp.int32), jax.ShapeDtypeStruct((bsz, tu, LANES), F32)]


def _tail_out_specs(d):
    return [_row_spec(d), _row_spec(d), _row_spec(LANES), _row_spec(LANES)]


def _attn_out(om, og, x, mod, wo, nf, rw, rb):
    bsz, tu, d = x.shape
    return pl.pallas_call(
        _attn_out_kernel,
        out_shape=_tail_out_shapes(bsz, tu, d),
        grid=(bsz, tu // TM),
        in_specs=[_row_spec(om.shape[-1]), _row_spec(og.shape[-1]), _row_spec(d), _mod_spec(d),
                  _full(wo.shape), _full(nf.shape), _full(rw.shape), _full(rb.shape)],
        out_specs=_tail_out_specs(d),
        compiler_params=_params("parallel", "parallel"),
        name="attn_out",
    )(om, og, x, mod, wo, nf, rw, rb)


def _head_sum(x, hs1_ref, hs2_ref):
    hi = x.astype(BF16)
    lo = (x - hi.astype(F32)).astype(BF16)
    s = _dot(hi, hs1_ref[...]) + _dot(lo, hs1_ref[...])
    shi = s.astype(BF16)
    slo = (s - shi.astype(F32)).astype(BF16)
    return _dot(shi, hs2_ref[...]) + _dot(slo, hs2_ref[...])


def _rwkv_feat_kernel(has_vres, nt, x_ref, xp_ref, xn_ref, mod_ref, nm_ref, mix_ref,
                      wr_ref, wk_ref, wv_ref, w0_ref, w1_ref, w2_ref, a0_ref, a1_ref, a2_ref,
                      g1_ref, g2_ref, kk_ref, ka_ref, rk_ref, hs1_ref, hs2_ref, *rest):
    if has_vres:
        v0_ref, v1_ref, v2_ref, vf_ref = rest[:4]
        rest = rest[4:]
    r_o, v_o, kk_o, w_o, k_o, b_o, g_o, bv_o = rest
    i = pl.program_id(1)
    mod = mod_ref[0, 0]
    d = x_ref.shape[-1]
    nm = nm_ref[...]

    def hmod(xx):
        return _rms(xx, nm, d) * (1.0 + mod[1:2]) + mod[0:1]

    h = hmod(x_ref[0])
    has_prev = (i >= 2).astype(F32)
    has_next = jnp.logical_and(i >= 1, i < nt - 1).astype(F32)
    prev_row = hmod(xp_ref[0])[SUBLANES - 1:SUBLANES] * has_prev
    next_row = hmod(xn_ref[0])[0:1] * has_next
    row = lax.broadcasted_iota(jnp.int32, h.shape, 0)
    hp = jnp.where(row == 0, prev_row, pltpu.roll(h, 1, axis=0))
    hn = jnp.where(row == TM - 1, next_row, pltpu.roll(h, TM - 1, axis=0))
    xx = 0.5 * (hp + hn) - h
    xr, xw, xk, xv, xa, xg = (h + xx * mix_ref[m:m + 1] for m in range(N_MIX))

    r = _bdot(xr, wr_ref)
    k = _bdot(xk, wk_ref)
    v = _bdot(xv, wv_ref)
    if has_vres:
        gate = _sigmoid(v0_ref[...] + _bdot(_bdot(xv, v1_ref), v2_ref))
        v = v + (vf_ref[0] - v) * gate
    kk = k * kk_ref[...]
    kk = kk * lax.rsqrt(jnp.maximum(_head_sum(kk * kk, hs1_ref, hs2_ref), 1e-24))
    tw = jnp.tanh(_bdot(xw, w1_ref)).astype(BF16)
    ta = _bdot(xa, a1_ref).astype(BF16)
    ksum = jnp.zeros_like(k)
    for dd in range(2):
        z = w0_ref[dd:dd + 1] + _dot(tw, w2_ref[dd])
        w_o[dd, 0] = -DECAY_SCALE * _sigmoid(z)
        a = _sigmoid(a0_ref[dd:dd + 1] + _dot(ta, a2_ref[dd]))
        kd = k * (1.0 + (a - 1.0) * ka_ref[...])
        k_o[dd, 0] = kd
        b_o[dd, 0] = kk * a
        ksum = ksum + kd
    g_o[0] = _bdot(_sigmoid(_bdot(xg, g1_ref)), g2_ref)
    bv_o[0] = _head_sum(r * ksum * rk_ref[...], hs1_ref, hs2_ref) * v
    r_o[0] = r
    v_o[0] = v
    kk_o[0] = kk


def _rwkv_feat(x, mod, nm, rp, v_first):
    bsz, tu, d = x.shape
    nt = tu // TM
    has_vres = v_first is not None
    nblk8 = tu // SUBLANES
    per8 = TM // SUBLANES
    prev_spec = pl.BlockSpec((1, SUBLANES, d), lambda b, i: (b, jnp.maximum(i * per8 - 1, 0), 0))
    next_spec = pl.BlockSpec((1, SUBLANES, d), lambda b, i: (b, jnp.minimum((i + 1) * per8, nblk8 - 1), 0))
    weights = [nm, rp["mix"], rp["w_r"], rp["w_k"], rp["w_v"], rp["w0"], rp["w1"], rp["w2"],
               rp["a0"], rp["a1"], rp["a2"], rp["g1"], rp["g2"], rp["k_k"], rp["k_a"], rp["r_k"],
               rp["hs1"], rp["hs2"]]
    extra, extra_specs = [], []
    if has_vres:
        extra = [rp["v0"], rp["v1"], rp["v2"], v_first]
        extra_specs = [_full(rp["v0"].shape), _full(rp["v1"].shape), _full(rp["v2"].shape), _row_spec(d)]
    one = jax.ShapeDtypeStruct((bsz, tu, d), F32)
    two = jax.ShapeDtypeStruct((2, bsz, tu, d), F32)
    dir_spec = pl.BlockSpec((2, 1, TM, d), lambda b, i: (0, b, i, 0))
    return pl.pallas_call(
        functools.partial(_rwkv_feat_kernel, has_vres, nt),
        out_shape=[one, one, one, two, two, two, one, one],
        grid=(bsz, nt),
        in_specs=[_row_spec(d), prev_spec, next_spec, _mod_spec(d)] + [_full(w.shape) for w in weights] + extra_specs,
        out_specs=[_row_spec(d)] * 3 + [dir_spec] * 3 + [_row_spec(d)] * 2,
        compiler_params=_params("parallel", "parallel"),
        name="rwkv_feat",
    )(x, x, x, mod, *weights, *extra)


def _dot_nt(a, b):
    return lax.dot_general(a, b, (((1,), (1,)), ((), ())), preferred_element_type=F32)


def _wkv_kernel(r_ref, v_ref, kk_ref, lw_ref, k_ref, b_ref, cum_ref, msk_ref, y_ref, st_scr):
    @pl.when(pl.program_id(2) == 0)
    def _():
        st_scr[...] = jnp.zeros_like(st_scr)

    lw = lw_ref[0, 0]
    lw_hi = lw.astype(BF16)
    lw_lo = (lw - lw_hi.astype(F32)).astype(BF16)
    cum = cum_ref[0]
    cs = _dot(cum, lw_hi) + _dot(cum, lw_lo)
    lam_inv = jnp.exp(-cs)
    rt = r_ref[0] * jnp.exp(cs)
    at = -kk_ref[0] * jnp.exp(cs - lw)
    kt = k_ref[0, 0] * lam_inv
    bt = b_ref[0, 0] * lam_inv
    v = v_ref[0]
    lam_tot = jnp.exp(jnp.sum(lw, axis=0, keepdims=True))

    strict = msk_ref[0, 0][:, :WKV_CHUNK] > 0.5
    m_ak = msk_ref[0, 1] > 0.5
    m_y = msk_ref[0, 2] > 0.5
    head_of_lane = lax.broadcasted_iota(jnp.int32, (1, LANES), 1) // RWKV_HEAD
    rowi = lax.broadcasted_iota(jnp.int32, (LANES, LANES), 0)
    coli = lax.broadcasted_iota(jnp.int32, (LANES, LANES), 1)
    blockdiag = (rowi // RWKV_HEAD) == (coli // RWKV_HEAD)

    n_pairs = lw.shape[-1] // LANES
    hpp = LANES // RWKV_HEAD
    n_sq = WKV_CHUNK.bit_length() - 1
    for p0 in range(0, n_pairs, WKV_PAIR_GROUP):
        pairs = list(range(p0, p0 + WKV_PAIR_GROUP))
        heads = [(p, hh) for p in pairs for hh in range(hpp)]
        sls = {p: slice(p * LANES, (p + 1) * LANES) for p in pairs}
        ar = {p: jnp.concatenate([at[:, sls[p]], rt[:, sls[p]]], axis=0) for p in pairs}
        bk = {p: jnp.concatenate([bt[:, sls[p]], kt[:, sls[p]]], axis=0).astype(BF16) for p in pairs}
        vp = {p: v[:, sls[p]] for p in pairs}
        vv = {p: jnp.concatenate([vp[p], vp[p]], axis=0).astype(BF16) for p in pairs}
        st = {p: st_scr[p] for p in pairs}
        st_b = {p: st[p].astype(BF16) for p in pairs}
        arm = {h: jnp.where(head_of_lane == h[1], ar[h[0]], 0.0).astype(BF16) for h in heads}
        g = {h: _dot_nt(arm[h], bk[h[0]]) for h in heads}
        s0 = {h: _dot_nt(arm[h], st_b[h[0]]) for h in heads}
        u = {h: _dot(jnp.where(m_ak, g[h][:WKV_CHUNK], 0.0).astype(BF16), vv[h[0]]) + s0[h][:WKV_CHUNK]
             for h in heads}
        pk = {h: jnp.where(strict, g[h][:WKV_CHUNK, :WKV_CHUNK], 0.0) for h in heads}
        for it in range(n_sq):
            pkb = {h: pk[h].astype(BF16) for h in heads}
            u = {h: u[h] + _dot(pkb[h], u[h].astype(BF16)) for h in heads}
            if it < n_sq - 1:
                pk = {h: _dot(pkb[h], pkb[h]) for h in heads}
        ys = {h: _dot(jnp.where(m_y, g[h][WKV_CHUNK:], 0.0).astype(BF16),
                      jnp.concatenate([u[h], vp[h[0]]], axis=0).astype(BF16)) + s0[h][WKV_CHUNK:]
              for h in heads}
        for p in pairs:
            y_ref[0, 0, :, sls[p]] = jnp.where(head_of_lane == 0, ys[(p, 0)], ys[(p, 1)])
            u_pair = jnp.where(head_of_lane == 0, u[(p, 0)], u[(p, 1)])
            uvp = jnp.concatenate([u_pair, vp[p]], axis=0).astype(BF16)
            upd = lax.dot_general(uvp, bk[p], (((0,), (0,)), ((), ())), preferred_element_type=F32)
            st_scr[p] = jnp.where(blockdiag, (st[p] + upd) * lam_tot[:, sls[p]], 0.0)


def _scan_tables():
    t = np.arange(WKV_CHUNK)
    incl = [t[None, :] <= t[:, None], t[None, :] >= t[:, None]]
    strict = [t[None, :] < t[:, None], t[None, :] > t[:, None]]
    msk = np.zeros((2, 3, WKV_CHUNK, 2 * WKV_CHUNK), np.float32)
    for dd in range(2):
        msk[dd, 0, :, :WKV_CHUNK] = strict[dd]
        msk[dd, 1, :, WKV_CHUNK:] = strict[dd]
        msk[dd, 2, :, :WKV_CHUNK] = incl[dd]
        msk[dd, 2, :, WKV_CHUNK:] = incl[dd]
    return jnp.asarray(np.stack(incl).astype(np.float32), BF16), jnp.asarray(msk, F32)


def _wkv_scan(r, v, kk, lw, k, b, n_ctx):
    bsz, tu, d = r.shape
    nb = tu // WKV_CHUNK
    cb = n_ctx // WKV_CHUNK
    cum, msk = _scan_tables()

    def blk(dd, j):
        back = jnp.where(j < cb, cb - 1 - j, nb - 1 + cb - j)
        return jnp.where(dd == 0, j, back)

    shared = pl.BlockSpec((1, WKV_CHUNK, d), lambda dd, bb, j: (bb, blk(dd, j), 0))
    per_dir = pl.BlockSpec((1, 1, WKV_CHUNK, d), lambda dd, bb, j: (dd, bb, blk(dd, j), 0))
    return pl.pallas_call(
        _wkv_kernel,
        out_shape=jax.ShapeDtypeStruct((2, bsz, tu, d), F32),
        grid=(2, bsz, nb),
        in_specs=[shared, shared, shared, per_dir, per_dir, per_dir,
                  pl.BlockSpec((1, WKV_CHUNK, WKV_CHUNK), lambda dd, bb, j: (dd, 0, 0)),
                  pl.BlockSpec((1, 3, WKV_CHUNK, 2 * WKV_CHUNK), lambda dd, bb, j: (dd, 0, 0, 0))],
        out_specs=per_dir,
        scratch_shapes=[pltpu.VMEM((d // LANES, LANES, LANES), F32)],
        compiler_params=_params("arbitrary", "arbitrary", "arbitrary"),
        name="wkv_scan",
    )(r, v, kk, lw, k, b, cum, msk)


def _rwkv_out_kernel(y_ref, g_ref, bv_ref, x_ref, mod_ref, lnw_ref, lnb_ref, wo_ref, hs1_ref, hs2_ref,
                     nf_ref, rw_ref, rb_ref, x_o, hf_o, idx_o, p_o):
    mod = mod_ref[0, 0]
    y = y_ref[0, 0] + y_ref[1, 0]
    inv_n = 1.0 / RWKV_HEAD
    mu = _head_sum(y, hs1_ref, hs2_ref) * inv_n
    dlt = y - mu
    var = _head_sum(dlt * dlt, hs1_ref, hs2_ref) * inv_n
    yn = dlt * lax.rsqrt(var + RWKV_GN_EPS) * lnw_ref[...] + lnb_ref[...]
    o = _bdot((yn + bv_ref[0]) * g_ref[0], wo_ref)
    x2 = x_ref[0] + mod[2:3] * o
    x_o[0] = x2
    _ffn_tail(x2, mod, nf_ref, rw_ref, rb_ref, hf_o, idx_o, p_o)


def _rwkv_out(y, g, bv, x, mod, rp, nf, rw, rb):
    bsz, tu, d = x.shape
    weights = [rp["ln_w"], rp["ln_b"], rp["w_o"], rp["hs1"], rp["hs2"], nf, rw, rb]
    return pl.pallas_call(
        _rwkv_out_kernel,
        out_shape=_tail_out_shapes(bsz, tu, d),
        grid=(bsz, tu // TM),
        in_specs=[pl.BlockSpec((2, 1, TM, d), lambda b, i: (0, b, i, 0))] + [_row_spec(d)] * 3 + [_mod_spec(d)]
        + [_full(w.shape) for w in weights],
        out_specs=_tail_out_specs(d),
        compiler_params=_params("parallel", "parallel"),
        name="rwkv_out",
    )(y, g, bv, x, mod, *weights)


def _dispatch_kernel(pos_ref, h_ref, init_hbm, out_hbm, hbuf, sem):
    del init_hbm
    h = h_ref[...]
    for s in range(SUBLANES):
        hbuf[:, s, :] = h[:, s * LANES:(s + 1) * LANES]

    def issue(jj, carry):
        for kk in range(TOP_K):
            pltpu.make_async_copy(hbuf.at[jj], out_hbm.at[pos_ref[jj * TOP_K + kk]], sem).start()
        return carry

    lax.fori_loop(0, TM, issue, 0)
    for _ in range(TOP_K):
        pltpu.make_async_copy(hbuf, out_hbm.at[pl.ds(0, TM)], sem).wait()


def _dispatch(pos_flat, h2, n_rows):
    n_tok, d = h2.shape
    init = jnp.zeros((n_rows, SUBLANES, LANES), h2.dtype)
    return pl.pallas_call(
        _dispatch_kernel,
        out_shape=jax.ShapeDtypeStruct(init.shape, init.dtype),
        grid=(n_tok // TM,),
        in_specs=[pl.BlockSpec((TM * TOP_K,), lambda i: (i,), memory_space=pltpu.SMEM),
                  pl.BlockSpec((TM, d), lambda i: (i, 0)), pl.BlockSpec(memory_space=pl.ANY)],
        out_specs=pl.BlockSpec(memory_space=pl.ANY),
        scratch_shapes=[pltpu.VMEM((TM, SUBLANES, LANES), h2.dtype), pltpu.SemaphoreType.DMA(())],
        input_output_aliases={2: 0},
        compiler_params=_params("arbitrary"),
        name="moe_dispatch",
    )(pos_flat, h2, init)


def _deinterleave_kernel(w_ref, p_ref, o_ref):
    o_ref[...] = _dot(w_ref[...].astype(BF16), p_ref[...]).astype(BF16)


def _deinterleave_w1(w1):
    f2 = w1.shape[-1]
    rows = int(np.prod(w1.shape[:-1]))
    col = jnp.arange(f2, dtype=jnp.int32)
    src = jnp.where(col < f2 // 2, 2 * col, 2 * (col - f2 // 2) + 1)
    perm = (col[:, None] == src[None, :]).astype(BF16)
    out = pl.pallas_call(
        _deinterleave_kernel,
        out_shape=jax.ShapeDtypeStruct((rows, f2), BF16),
        grid=(rows // TM_FFN,),
        in_specs=[pl.BlockSpec((TM_FFN, f2), lambda i: (i, 0)), pl.BlockSpec((f2, f2), lambda i: (0, 0))],
        out_specs=pl.BlockSpec((TM_FFN, f2), lambda i: (i, 0)),
        compiler_params=_params("parallel"),
        name="w1_deinterleave",
    )(w1.reshape(rows, f2), perm)
    return out.reshape(w1.shape)


def _rows_from_tiles(ref):
    return jnp.concatenate([ref[:, s, :] for s in range(SUBLANES)], axis=1)


def _moe_ffn_kernel(te_ref, nu_ref, hs_ref, w1_ref, b1_ref, w2_ref, b2_ref, ys_ref):
    t = pl.program_id(0)

    @pl.when(t < nu_ref[0])
    def _():
        x = _rows_from_tiles(hs_ref).astype(BF16)
        u = _dot(x, w1_ref[0]) + b1_ref[0]
        f = u.shape[-1] // 2
        glu = jnp.minimum(u[:, :f], SWIGLU_LIMIT)
        lin = jnp.clip(u[:, f:], -SWIGLU_LIMIT, SWIGLU_LIMIT)
        act = glu * _sigmoid(SWIGLU_ALPHA * glu) * (lin + 1.0)
        y = _dot(act.astype(BF16), w2_ref[0]) + b2_ref[0]
        for s in range(SUBLANES):
            ys_ref[:, s, :] = y[:, s * LANES:(s + 1) * LANES]

    @pl.when(t >= nu_ref[0])
    def _():
        ys_ref[...] = jnp.zeros_like(ys_ref)


def _moe_ffn(tile_expert, n_used, hs3, w1, b1, w2, b2):
    n_rows = hs3.shape[0]
    n_tiles = n_rows // TM_FFN
    d = w1.shape[1]
    f2 = w1.shape[2]
    row_map = lambda t, te, nu: (jnp.minimum(t, nu[0] - 1), 0, 0)
    exp_map = lambda t, te, nu: (te[t], 0, 0)
    grid_spec = pltpu.PrefetchScalarGridSpec(
        num_scalar_prefetch=2,
        grid=(n_tiles,),
        in_specs=[pl.BlockSpec((TM_FFN, SUBLANES, LANES), row_map),
                  pl.BlockSpec((1, d, f2), exp_map), pl.BlockSpec((1, 1, f2), exp_map),
                  pl.BlockSpec((1, f2 // 2, d), exp_map), pl.BlockSpec((1, 1, d), exp_map)],
        out_specs=pl.BlockSpec((TM_FFN, SUBLANES, LANES), lambda t, te, nu: (t, 0, 0)),
    )
    return pl.pallas_call(
        _moe_ffn_kernel,
        out_shape=jax.ShapeDtypeStruct(hs3.shape, F32),
        grid_spec=grid_spec,
        compiler_params=_params("arbitrary"),
        name="moe_ffn",
    )(tile_expert, n_used, hs3, w1, b1, w2, b2)


def _combine_kernel(pos_ref, ys_hbm, x_ref, p_ref, mod_ref, x_o, ybuf, sem):
    def issue(jj, carry):
        for kk in range(TOP_K):
            pltpu.make_async_copy(ys_hbm.at[pos_ref[jj * TOP_K + kk]], ybuf.at[kk * TM + jj], sem).start()
        return carry

    lax.fori_loop(0, TM, issue, 0)
    n = TM * TOP_K
    pltpu.make_async_copy(ys_hbm.at[pl.ds(0, n)], ybuf, sem).wait()
    p = p_ref[0]
    acc = jnp.zeros(x_ref.shape[1:], F32)
    for kk in range(TOP_K):
        yk = jnp.concatenate([ybuf[pl.ds(kk * TM, TM), s, :] for s in range(SUBLANES)], axis=1)
        acc = acc + p[:, kk:kk + 1] * yk
    x_o[0] = x_ref[0] + mod_ref[0, 0][5:6] * acc


def _combine(pos_flat, ys3, x, p_pad, mod):
    bsz, tu, d = x.shape
    nt = tu // TM
    return pl.pallas_call(
        _combine_kernel,
        out_shape=jax.ShapeDtypeStruct(x.shape, F32),
        grid=(bsz, nt),
        in_specs=[pl.BlockSpec((TM * TOP_K,), lambda b, i: (b * nt + i,), memory_space=pltpu.SMEM),
                  pl.BlockSpec(memory_space=pl.ANY), _row_spec(d), _row_spec(LANES), _mod_spec(d)],
        out_specs=_row_spec(d),
        scratch_shapes=[pltpu.VMEM((TM * TOP_K, SUBLANES, LANES), F32), pltpu.SemaphoreType.DMA(())],
        compiler_params=_params("arbitrary", "arbitrary"),
        name="moe_combine",
    )(pos_flat, ys3, x, p_pad, mod)


def _moe(x, hf, idx_pad, p_pad, mod, w1, b1, w2, b2):
    bsz, tu, d = x.shape
    n_tok = bsz * tu
    n_rows = n_tok * TOP_K + N_EXPERTS * TM_FFN
    n_tiles = n_rows // TM_FFN
    idx = idx_pad.reshape(n_tok, LANES)[:, :TOP_K]
    onehot = jnp.sum((idx[:, :, None] == jnp.arange(N_EXPERTS, dtype=jnp.int32)).astype(jnp.int32), axis=1)
    csum = jnp.cumsum(onehot, axis=0)
    counts = csum[-1]
    padded = ((counts + TM_FFN - 1) // TM_FFN) * TM_FFN
    ends = jnp.cumsum(padded)
    pos = (ends - padded)[idx] + jnp.take_along_axis(csum - onehot, idx, axis=1)
    n_used = (ends[-1] // TM_FFN).astype(jnp.int32)
    tile_start = jnp.arange(n_tiles, dtype=jnp.int32) * TM_FFN
    tile_expert = jnp.sum((tile_start[:, None] >= ends[None, :]).astype(jnp.int32), axis=1)
    last = jnp.take(tile_expert, jnp.maximum(n_used - 1, 0))
    tile_expert = jnp.where(jnp.arange(n_tiles) < n_used, tile_expert, last).astype(jnp.int32)
    pos_flat = pos.reshape(-1).astype(jnp.int32)

    hs3 = _dispatch(pos_flat, hf.reshape(n_tok, d), n_rows)
    ys3 = _moe_ffn(tile_expert, n_used.reshape(1), hs3, w1, b1, w2, b2)
    return _combine(pos_flat, ys3, x, p_pad, mod)


def _rope_tables(n_ctx, n_lat):
    def angles(rot_dim):
        rows = n_lat // GRID_W
        row = jnp.repeat(jnp.arange(rows, dtype=F32), GRID_W)
        col = jnp.tile(jnp.arange(GRID_W, dtype=F32), rows)
        half = rot_dim // 2
        inv_freq = ROPE_THETA ** (-jnp.arange(0, half, 2, dtype=F32) / half)
        return row[:, None] * inv_freq[None, :], col[:, None] * inv_freq[None, :]

    def with_ctx(t, fill):
        return jnp.concatenate([jnp.full((n_ctx, LANES), fill, F32), t], axis=0)

    ar, ac = angles(MLA_ROPE)
    one = jnp.ones((n_lat, 32), F32)
    zero = jnp.zeros((n_lat, 32), F32)
    cm = jnp.concatenate([jnp.cos(ar), jnp.cos(ac), one, jnp.cos(ar), jnp.cos(ac), one], axis=1)
    sm = jnp.concatenate([-jnp.sin(ar), -jnp.sin(ac), zero, jnp.sin(ar), jnp.sin(ac), zero], axis=1)
    ar, ac = angles(GQA_HEAD_DIM)
    cg = jnp.concatenate([jnp.cos(ar), jnp.cos(ac), jnp.cos(ar), jnp.cos(ac)], axis=1)
    sg = jnp.concatenate([-jnp.sin(ar), -jnp.sin(ac), jnp.sin(ar), jnp.sin(ac)], axis=1)
    return with_ctx(cm, 1.0), with_ctx(sm, 0.0), with_ctx(cg, 1.0), with_ctx(sg, 0.0)


_PERM_GQA = np.concatenate([np.arange(0, 32), np.arange(64, 96), np.arange(32, 64), np.arange(96, 128)])
_SRC_MLA = np.zeros(LANES, np.int32)
_MASK_MLA = np.zeros(LANES, np.float32)
_SRC_MLA[0:16], _SRC_MLA[16:32], _SRC_MLA[64:80], _SRC_MLA[80:96] = (
    np.arange(0, 16), np.arange(32, 48), np.arange(16, 32), np.arange(48, 64))
_MASK_MLA[0:32] = 1.0
_MASK_MLA[64:96] = 1.0


def _pad_rope64(w):
    return jnp.take(w, _SRC_MLA, axis=-1) * _MASK_MLA


def _attn_weights(e, w_in, q_norm, w_uq, kv_norm, w_ukv, qn_g, qr_g, kn_g, kr_g, gq_g, gk_g, w_out):
    s1 = MLA_Q_RANK
    s2 = s1 + MLA_KV_RANK
    s3 = s2 + MLA_ROPE
    s4 = s3 + GQA_HEADS * GQA_HEAD_DIM
    s5 = s4 + GQA_KV_HEADS * GQA_HEAD_DIM
    wi = w_in[e]
    d = wi.shape[0]
    gq = wi[:, s3:s4].reshape(d, GQA_HEADS, GQA_HEAD_DIM)[:, :, _PERM_GQA].reshape(d, -1)
    gk = wi[:, s4:s5].reshape(d, GQA_KV_HEADS, GQA_HEAD_DIM)[:, :, _PERM_GQA].reshape(d, -1)
    w_in_p = jnp.concatenate([wi[:, :s2], _pad_rope64(wi[:, s2:s3]), gq, gk, wi[:, s5:]], axis=1)
    uq = w_uq[e].reshape(MLA_Q_RANK, MLA_HEADS, MLA_NOPE + MLA_ROPE)
    uq_p = jnp.concatenate([uq[:, :, :MLA_NOPE], _pad_rope64(uq[:, :, MLA_NOPE:])], axis=-1)
    row = lambda g: g.reshape(1, -1).astype(F32)
    return dict(
        w_in=w_in_p.astype(BF16), q_norm=row(q_norm[e]), w_uq=uq_p.reshape(MLA_Q_RANK, -1).astype(BF16),
        kv_norm=row(kv_norm[e]), w_ukv=w_ukv[e].astype(BF16),
        qn_g=row(qn_g[e]), qr_g=row(_pad_rope64(qr_g[e])), kn_g=row(kn_g[e]), kr_g=row(_pad_rope64(kr_g[e])),
        gq_g=row(gq_g[e][_PERM_GQA]), gk_g=row(gk_g[e][_PERM_GQA]), w_out=w_out[e].astype(BF16))


def _pad_cols(w, n):
    return jnp.pad(w, ((0, 0), (0, n - w.shape[1])))


def _pad_rows(w, n):
    return jnp.pad(w, ((0, n - w.shape[0]), (0, 0)))


def _lora_pair(w1, w2):
    rank = w1.shape[-1]
    down = jnp.concatenate([w1[0], w1[1]], axis=1)
    up = jnp.stack([jnp.pad(w2[0], ((0, rank), (0, 0))), jnp.pad(w2[1], ((rank, 0), (0, 0)))])
    return down.astype(BF16), up.astype(BF16)


def _rwkv_weights(j, mix, w_r, w_k, w_v, w_o, w0, w1, w2, a0, a1, a2, v0, v1, v2, g1, g2, k_k, k_a, r_k, ln_w, ln_b):
    d = w_r.shape[1]
    row = lambda g: g.reshape(1, -1).astype(F32)
    heads = np.arange(d) // RWKV_HEAD
    hs1 = (heads[:, None] == np.arange(LANES)[None, :]).astype(np.float32)
    w1c, w2p = _lora_pair(w1[j], w2[j])
    a1c, a2p = _lora_pair(a1[j], a2[j])
    rp = dict(mix=mix[j], w_r=w_r[j].astype(BF16), w_k=w_k[j].astype(BF16), w_v=w_v[j].astype(BF16),
              w_o=w_o[j].astype(BF16), w0=w0[j], w1=w1c, w2=w2p, a0=a0[j], a1=a1c, a2=a2p,
              g1=g1[j].astype(BF16), g2=g2[j].astype(BF16), k_k=row(k_k[j]), k_a=row(k_a[j]), r_k=row(r_k[j]),
              ln_w=row(ln_w[j]), ln_b=row(ln_b[j]),
              hs1=jnp.asarray(hs1, BF16), hs2=jnp.asarray(hs1.T, BF16))
    if j > 0:
        rp.update(v0=row(v0[j - 1]), v1=_pad_cols(v1[j - 1], LANES).astype(BF16),
                  v2=_pad_rows(v2[j - 1], LANES).astype(BF16))
    return rp


def kernel(x, c, ctx, c_ctx, ada_w, ada_b, norm_mix, norm_ffn, attn_w_in, mla_q_norm, mla_w_uq, mla_kv_norm, mla_w_ukv, mla_qn_g, mla_qr_g, mla_kn_g, mla_kr_g, gqa_q_g, gqa_k_g, attn_w_out, rwkv_mix, rwkv_w_r, rwkv_w_k, rwkv_w_v, rwkv_w_o, rwkv_w0, rwkv_w1, rwkv_w2, rwkv_a0, rwkv_a1, rwkv_a2, rwkv_v0, rwkv_v1, rwkv_v2, rwkv_g1, rwkv_g2, rwkv_k_k, rwkv_k_a, rwkv_r_k, rwkv_ln_w, rwkv_ln_b, moe_router_w, moe_router_b, moe_w1, moe_b1, moe_w2, moe_b2):
    bsz, n_lat, d = x.shape
    n_ctx = ctx.shape[1]
    depth = ada_w.shape[0]
    tu = n_ctx + n_lat
    assert n_ctx % TM == 0 and n_lat % TM == 0 and n_ctx % WKV_CHUNK == 0 and d == SUBLANES * LANES
    assert (bsz * tu * TOP_K) % TM_FFN == 0

    rows = -(-(bsz + 1) // SUBLANES) * SUBLANES
    cond = jnp.zeros((rows, d), F32).at[:bsz].set(c).at[bsz].set(c_ctx)
    mods = _adaln(cond, ada_w, ada_b)
    mods_lat = mods[:, :bsz].reshape(depth, bsz, 1, N_MOD, d)
    mods_ctx = jnp.broadcast_to(mods[:, bsz].reshape(depth, 1, 1, N_MOD, d), mods_lat.shape)
    mods = jnp.concatenate([mods_ctx, mods_lat], axis=2)

    tabs = _rope_tables(n_ctx, n_lat)
    xs = jnp.concatenate([ctx, x], axis=1)

    n_e, _, f2 = moe_w1.shape[1:]
    rw_all = jnp.pad(moe_router_w, ((0, 0), (0, 0), (0, LANES - n_e)))
    rb_all = jnp.pad(moe_router_b, ((0, 0), (0, LANES - n_e)), constant_values=-1e30).reshape(depth, 1, LANES)

    w1_all = _deinterleave_w1(moe_w1)

    v_first = None
    for i in range(depth):
        mod = mods[i]
        nm = norm_mix[i].reshape(1, d)
        nf = norm_ffn[i].reshape(1, d)
        rw, rb = rw_all[i], rb_all[i]
        if i % 2 == 0:
            aw = _attn_weights(i // 2, attn_w_in, mla_q_norm, mla_w_uq, mla_kv_norm, mla_w_ukv, mla_qn_g,
                               mla_qr_g, mla_kn_g, mla_kr_g, gqa_q_g, gqa_k_g, attn_w_out)
            qm, km, vm, qg, kg, vg = _attn_proj(xs, mod, nm, aw, tabs)
            om = _attention(qm, km, vm, n_ctx)
            og = _attention(qg, kg, vg, n_ctx)
            xs, hf, idx_pad, p_pad = _attn_out(om, og, xs, mod, aw["w_out"], nf, rw, rb)
        else:
            j = i // 2
            rp = _rwkv_weights(j, rwkv_mix, rwkv_w_r, rwkv_w_k, rwkv_w_v, rwkv_w_o, rwkv_w0, rwkv_w1, rwkv_w2,
                               rwkv_a0, rwkv_a1, rwkv_a2, rwkv_v0, rwkv_v1, rwkv_v2, rwkv_g1, rwkv_g2,
                               rwkv_k_k, rwkv_k_a, rwkv_r_k, rwkv_ln_w, rwkv_ln_b)
            r, v, kk, w, k, b, g, bv = _rwkv_feat(xs, mod, nm, rp, v_first if j > 0 else None)
            if j == 0:
                v_first = v
            y = _wkv_scan(r, v, kk, w, k, b, n_ctx)
            xs, hf, idx_pad, p_pad = _rwkv_out(y, g, bv, xs, mod, rp, nf, rw, rb)
        w1 = w1_all[i]
        b1 = jnp.concatenate([moe_b1[i][:, 0::2], moe_b1[i][:, 1::2]], axis=-1).reshape(n_e, 1, f2)
        w2 = moe_w2[i].astype(BF16)
        b2 = moe_b2[i].reshape(n_e, 1, d)
        xs = _moe(xs, hf, idx_pad, p_pad, mod, w1, b1, w2, b2)
    return xs[:, n_ctx:]
```

```python
import functools

import jax
import jax.numpy as jnp
import numpy as np
from jax import lax
from jax.experimental import pallas as pl
from jax.experimental.pallas import tpu as pltpu

F32 = jnp.float32
BF16 = jnp.bfloat16
HIGHEST = lax.Precision.HIGHEST

SUBLANES = 8
LANES = 128
VMEM_LIMIT = 56 * 1024 * 1024

GRID_W = 64
ROPE_THETA = 10000.0
NORM_EPS = 1e-6
N_MOD = 6

MLA_HEADS = 4
MLA_Q_RANK = 384
MLA_KV_RANK = 256
MLA_NOPE = 128
MLA_ROPE = 64
MLA_V = 128
MLA_SCALE = (MLA_NOPE + MLA_ROPE) ** -0.5
GQA_HEADS = 4
GQA_KV_HEADS = 2
GQA_HEAD_DIM = 128
GQA_SCALE = GQA_HEAD_DIM ** -0.5
LOG2E = float(np.log2(np.e))

RWKV_HEAD = 64
RWKV_GN_EPS = 64e-5
N_MIX = 6
DECAY_SCALE = float(np.exp(-0.5))

N_EXPERTS = 32
TOP_K = 4
SWIGLU_ALPHA = 1.702
SWIGLU_LIMIT = 7.0

TM = 256
TM_FFN = 512
WKV_CHUNK = 64
WKV_ROWS = 2


def _params(*sem):
    return pltpu.CompilerParams(dimension_semantics=sem, vmem_limit_bytes=VMEM_LIMIT)


def _dot(a, b):
    return jnp.dot(a, b, preferred_element_type=F32)


def _bdot(a, b_ref):
    return jnp.dot(a.astype(BF16), b_ref[...], preferred_element_type=F32)


def _rms(x, g, n):
    ms = jnp.sum(x * x, axis=-1, keepdims=True) * (1.0 / n)
    return x * lax.rsqrt(ms + NORM_EPS) * g


def _sigmoid(x):
    return 1.0 / (1.0 + jnp.exp(-x))


def _rope(x, cos, sin):
    return x * cos + pltpu.roll(x, 64, axis=1) * sin


def _adaln_kernel(c_ref, w_ref, b_ref, o_ref):
    c = c_ref[...]
    s = c * _sigmoid(c)
    o_ref[0] = jnp.dot(s, w_ref[0], precision=HIGHEST, preferred_element_type=F32) + b_ref[0]


def _adaln(cond, ada_w, ada_b):
    depth, d, nd = ada_w.shape
    rows = cond.shape[0]
    return pl.pallas_call(
        _adaln_kernel,
        out_shape=jax.ShapeDtypeStruct((depth, rows, nd), F32),
        grid=(depth, nd // d),
        in_specs=[pl.BlockSpec((rows, d), lambda l, n: (0, 0)),
                  pl.BlockSpec((1, d, d), lambda l, n: (l, 0, n)),
                  pl.BlockSpec((1, 1, d), lambda l, n: (l, 0, n))],
        out_specs=pl.BlockSpec((1, rows, d), lambda l, n: (l, 0, n)),
        compiler_params=_params("arbitrary", "arbitrary"),
        name="adaln",
    )(cond, ada_w, ada_b.reshape(depth, 1, nd))


def _ffn_tail(x2, mod, nf_ref, rw_ref, rb_ref, hf_o, idx_o, p_o):
    d = x2.shape[-1]
    hf = _rms(x2, nf_ref[...], d) * (1.0 + mod[4:5]) + mod[3:4]
    hf_o[0] = hf
    logits = jnp.dot(hf, rw_ref[...], precision=HIGHEST, preferred_element_type=F32) + rb_ref[...]
    lane = lax.broadcasted_iota(jnp.int32, logits.shape, 1).astype(F32)
    vals, idxs = [], []
    l = logits
    for _ in range(TOP_K):
        m = jnp.max(l, axis=-1, keepdims=True)
        ik = jnp.min(jnp.where(l == m, lane, float(LANES)), axis=-1, keepdims=True)
        vals.append(m)
        idxs.append(ik)
        l = jnp.where(lane == ik, -jnp.inf, l)
    es = [jnp.exp(v - vals[0]) for v in vals]
    inv = 1.0 / (es[0] + es[1] + es[2] + es[3])
    idx_acc = jnp.zeros_like(logits)
    p_acc = jnp.zeros_like(logits)
    for k in range(TOP_K):
        idx_acc = jnp.where(lane == float(k), idxs[k], idx_acc)
        p_acc = jnp.where(lane == float(k), es[k] * inv, p_acc)
    idx_o[0] = idx_acc.astype(jnp.int32)
    p_o[0] = p_acc


def _attn_proj_kernel(x_ref, mod_ref, nm_ref, win_ref, qn_ref, wuq_ref, kvn_ref, wukv_ref,
                      qng_ref, qrg_ref, kng_ref, krg_ref, gqg_ref, gkg_ref,
                      cm_ref, sm_ref, cg_ref, sg_ref,
                      qm_o, km_o, vm_o, qg_o, kg_o, vg_o):
    x = x_ref[0]
    mod = mod_ref[0, 0]
    d = x.shape[-1]
    h = _rms(x, nm_ref[...], d) * (1.0 + mod[1:2]) + mod[0:1]
    z = _bdot(h, win_ref)
    o1 = MLA_Q_RANK
    o2 = o1 + MLA_KV_RANK
    o3 = o2 + LANES
    o4 = o3 + GQA_HEADS * GQA_HEAD_DIM
    o5 = o4 + GQA_KV_HEADS * GQA_HEAD_DIM
    q = _bdot(_rms(z[:, :o1], qn_ref[...], MLA_Q_RANK), wuq_ref)
    kv = _bdot(_rms(z[:, o1:o2], kvn_ref[...], MLA_KV_RANK), wukv_ref)
    cm, sm, cg, sg = cm_ref[...], sm_ref[...], cg_ref[...], sg_ref[...]
    kr = _rope(_rms(z[:, o2:o3], krg_ref[...], MLA_ROPE), cm, sm)
    for hh in range(MLA_HEADS):
        b0 = hh * 2 * LANES
        qn = _rms(q[:, b0:b0 + LANES], qng_ref[...], MLA_NOPE)
        qr = _rope(_rms(q[:, b0 + LANES:b0 + 2 * LANES], qrg_ref[...], MLA_ROPE), cm, sm)
        qm_o[0, hh] = (jnp.concatenate([qn, qr], axis=1) * (MLA_SCALE * LOG2E)).astype(BF16)
        kn = _rms(kv[:, b0:b0 + LANES], kng_ref[...], MLA_NOPE)
        km_o[0, hh] = jnp.concatenate([kn, kr], axis=1).astype(BF16)
        vm_o[0, hh] = kv[:, b0 + LANES:b0 + 2 * LANES].astype(BF16)
    for hh in range(GQA_HEADS):
        g = z[:, o3 + hh * LANES:o3 + (hh + 1) * LANES]
        qg_o[0, hh] = (_rope(_rms(g, gqg_ref[...], GQA_HEAD_DIM), cg, sg) * (GQA_SCALE * LOG2E)).astype(BF16)
    for hh in range(GQA_KV_HEADS):
        g = z[:, o4 + hh * LANES:o4 + (hh + 1) * LANES]
        kg_o[0, hh] = _rope(_rms(g, gkg_ref[...], GQA_HEAD_DIM), cg, sg).astype(BF16)
        vg_o[0, hh] = z[:, o5 + hh * LANES:o5 + (hh + 1) * LANES].astype(BF16)


def _full(shape):
    nd = len(shape)
    return pl.BlockSpec(shape, lambda b, i: (0,) * nd)


def _mod_spec(d):
    return pl.BlockSpec((1, 1, N_MOD, d), lambda b, i: (b, jnp.minimum(i, 1), 0, 0))


def _row_spec(d):
    return pl.BlockSpec((1, TM, d), lambda b, i: (b, i, 0))


def _attn_proj(x, mod, nm, aw, tabs):
    bsz, tu, d = x.shape
    nt = tu // TM
    head_spec = lambda nh, w: pl.BlockSpec((1, nh, TM, w), lambda b, i: (b, 0, i, 0))
    tab_spec = pl.BlockSpec((TM, LANES), lambda b, i: (i, 0))
    weights = [nm, aw["w_in"], aw["q_norm"], aw["w_uq"], aw["kv_norm"], aw["w_ukv"],
               aw["qn_g"], aw["qr_g"], aw["kn_g"], aw["kr_g"], aw["gq_g"], aw["gk_g"]]
    out_shape = [
        jax.ShapeDtypeStruct((bsz, MLA_HEADS, tu, 2 * LANES), BF16),
        jax.ShapeDtypeStruct((bsz, MLA_HEADS, tu, 2 * LANES), BF16),
        jax.ShapeDtypeStruct((bsz, MLA_HEADS, tu, LANES), BF16),
        jax.ShapeDtypeStruct((bsz, GQA_HEADS, tu, LANES), BF16),
        jax.ShapeDtypeStruct((bsz, GQA_KV_HEADS, tu, LANES), BF16),
        jax.ShapeDtypeStruct((bsz, GQA_KV_HEADS, tu, LANES), BF16),
    ]
    return pl.pallas_call(
        _attn_proj_kernel,
        out_shape=out_shape,
        grid=(bsz, nt),
        in_specs=[_row_spec(d), _mod_spec(d)] + [_full(w.shape) for w in weights] + [tab_spec] * 4,
        out_specs=[head_spec(MLA_HEADS, 2 * LANES), head_spec(MLA_HEADS, 2 * LANES),
                   head_spec(MLA_HEADS, LANES), head_spec(GQA_HEADS, LANES),
                   head_spec(GQA_KV_HEADS, LANES), head_spec(GQA_KV_HEADS, LANES)],
        compiler_params=_params("parallel", "parallel"),
        name="attn_proj",
    )(x, mod, *weights, *tabs)


def _attn_kernel(n_ctx, q_ref, k_ref, v_ref, o_ref):
    q = q_ref[0, 0]

    def attend(k, v):
        s = lax.dot_general(q, k, (((1,), (1,)), ((), ())), preferred_element_type=F32)
        m = jnp.max(s, axis=-1, keepdims=True)
        p = jnp.exp2(s - m)
        l = jnp.sum(p, axis=-1, keepdims=True)
        o = _dot(p.astype(BF16), v)
        o_ref[0] = (o * (1.0 / l)).astype(o_ref.dtype)

    is_ctx = pl.program_id(2) == 0

    @pl.when(is_ctx)
    def _():
        attend(k_ref[0, 0, :n_ctx], v_ref[0, 0, :n_ctx])

    @pl.when(jnp.logical_not(is_ctx))
    def _():
        attend(k_ref[0, 0], v_ref[0, 0])


def _attention(q, k, v, n_ctx):
    bsz, hq, tu, dk = q.shape
    hk = k.shape[1]
    grp = hq // hk
    dv = v.shape[-1]
    return pl.pallas_call(
        functools.partial(_attn_kernel, n_ctx),
        out_shape=jax.ShapeDtypeStruct((bsz, tu, hq * dv), BF16),
        grid=(bsz, hq, tu // TM),
        in_specs=[pl.BlockSpec((1, 1, TM, dk), lambda b, h, i: (b, h, i, 0)),
                  pl.BlockSpec((1, 1, tu, dk), lambda b, h, i: (b, h // grp, 0, 0)),
                  pl.BlockSpec((1, 1, tu, dv), lambda b, h, i: (b, h // grp, 0, 0))],
        out_specs=pl.BlockSpec((1, TM, dv), lambda b, h, i: (b, i, h)),
        compiler_params=_params("parallel", "parallel", "arbitrary"),
        name="attention",
    )(q, k, v)


def _attn_out_kernel(om_ref, og_ref, x_ref, mod_ref, wo_ref, nf_ref, rw_ref, rb_ref,
                     x_o, hf_o, idx_o, p_o):
    mod = mod_ref[0, 0]
    nm = om_ref.shape[-1]
    o = _dot(om_ref[0], wo_ref[:nm]) + _dot(og_ref[0], wo_ref[nm:])
    x2 = x_ref[0] + mod[2:3] * o
    x_o[0] = x2
    _ffn_tail(x2, mod, nf_ref, rw_ref, rb_ref, hf_o, idx_o, p_o)


def _tail_out_shapes(bsz, tu, d):
    return [jax.ShapeDtypeStruct((bsz, tu, d), F32), jax.ShapeDtypeStruct((bsz, tu, d), F32),
            jax.ShapeDtypeStruct((bsz, tu, LANES), jnp.int32), jax.ShapeDtypeStruct((bsz, tu, LANES), F32)]


def _tail_out_specs(d):
    return [_row_spec(d), _row_spec(d), _row_spec(LANES), _row_spec(LANES)]


def _attn_out(om, og, x, mod, wo, nf, rw, rb):
    bsz, tu, d = x.shape
    return pl.pallas_call(
        _attn_out_kernel,
        out_shape=_tail_out_shapes(bsz, tu, d),
        grid=(bsz, tu // TM),
        in_specs=[_row_spec(om.shape[-1]), _row_spec(og.shape[-1]), _row_spec(d), _mod_spec(d),
                  _full(wo.shape), _full(nf.shape), _full(rw.shape), _full(rb.shape)],
        out_specs=_tail_out_specs(d),
        compiler_params=_params("parallel", "parallel"),
        name="attn_out",
    )(om, og, x, mod, wo, nf, rw, rb)


def _head_sum(x, hs1_ref, hs2_ref):
    hi = x.astype(BF16)
    lo = (x - hi.astype(F32)).astype(BF16)
    s = _dot(hi, hs1_ref[...]) + _dot(lo, hs1_ref[...])
    shi = s.astype(BF16)
    slo = (s - shi.astype(F32)).astype(BF16)
    return _dot(shi, hs2_ref[...]) + _dot(slo, hs2_ref[...])


def _rwkv_feat_kernel(has_vres, nt, x_ref, xp_ref, xn_ref, mod_ref, nm_ref, mix_ref,
                      wr_ref, wk_ref, wv_ref, w0_ref, w1_ref, w2_ref, a0_ref, a1_ref, a2_ref,
                      g1_ref, g2_ref, kk_ref, ka_ref, rk_ref, hs1_ref, hs2_ref, *rest):
    if has_vres:
        v0_ref, v1_ref, v2_ref, vf_ref = rest[:4]
        rest = rest[4:]
    r_o, v_o, kk_o, w_o, k_o, b_o, g_o, bv_o = rest
    i = pl.program_id(1)
    mod = mod_ref[0, 0]
    d = x_ref.shape[-1]
    nm = nm_ref[...]

    def hmod(xx):
        return _rms(xx, nm, d) * (1.0 + mod[1:2]) + mod[0:1]

    h = hmod(x_ref[0])
    has_prev = (i >= 2).astype(F32)
    has_next = jnp.logical_and(i >= 1, i < nt - 1).astype(F32)
    prev_row = hmod(xp_ref[0])[SUBLANES - 1:SUBLANES] * has_prev
    next_row = hmod(xn_ref[0])[0:1] * has_next
    row = lax.broadcasted_iota(jnp.int32, h.shape, 0)
    hp = jnp.where(row == 0, prev_row, pltpu.roll(h, 1, axis=0))
    hn = jnp.where(row == TM - 1, next_row, pltpu.roll(h, TM - 1, axis=0))
    xx = 0.5 * (hp + hn) - h
    xr, xw, xk, xv, xa, xg = (h + xx * mix_ref[m:m + 1] for m in range(N_MIX))

    r = _bdot(xr, wr_ref)
    k = _bdot(xk, wk_ref)
    v = _bdot(xv, wv_ref)
    if has_vres:
        gate = _sigmoid(v0_ref[...] + _bdot(_bdot(xv, v1_ref), v2_ref))
        v = v + (vf_ref[0] - v) * gate
    kk = k * kk_ref[...]
    kk = kk * lax.rsqrt(jnp.maximum(_head_sum(kk * kk, hs1_ref, hs2_ref), 1e-24))
    tw = jnp.tanh(_bdot(xw, w1_ref)).astype(BF16)
    ta = _bdot(xa, a1_ref).astype(BF16)
    ksum = jnp.zeros_like(k)
    for dd in range(2):
        z = w0_ref[dd:dd + 1] + _dot(tw, w2_ref[dd])
        w_o[dd, 0] = -DECAY_SCALE * _sigmoid(z)
        a = _sigmoid(a0_ref[dd:dd + 1] + _dot(ta, a2_ref[dd]))
        kd = k * (1.0 + (a - 1.0) * ka_ref[...])
        k_o[dd, 0] = kd
        b_o[dd, 0] = kk * a
        ksum = ksum + kd
    g_o[0] = _bdot(_sigmoid(_bdot(xg, g1_ref)), g2_ref)
    bv_o[0] = _head_sum(r * ksum * rk_ref[...], hs1_ref, hs2_ref) * v
    r_o[0] = r
    v_o[0] = v
    kk_o[0] = kk


def _rwkv_feat(x, mod, nm, rp, v_first):
    bsz, tu, d = x.shape
    nt = tu // TM
    has_vres = v_first is not None
    nblk8 = tu // SUBLANES
    per8 = TM // SUBLANES
    prev_spec = pl.BlockSpec((1, SUBLANES, d), lambda b, i: (b, jnp.maximum(i * per8 - 1, 0), 0))
    next_spec = pl.BlockSpec((1, SUBLANES, d), lambda b, i: (b, jnp.minimum((i + 1) * per8, nblk8 - 1), 0))
    weights = [nm, rp["mix"], rp["w_r"], rp["w_k"], rp["w_v"], rp["w0"], rp["w1"], rp["w2"],
               rp["a0"], rp["a1"], rp["a2"], rp["g1"], rp["g2"], rp["k_k"], rp["k_a"], rp["r_k"],
               rp["hs1"], rp["hs2"]]
    extra, extra_specs = [], []
    if has_vres:
        extra = [rp["v0"], rp["v1"], rp["v2"], v_first]
        extra_specs = [_full(rp["v0"].shape), _full(rp["v1"].shape), _full(rp["v2"].shape), _row_spec(d)]
    one = jax.ShapeDtypeStruct((bsz, tu, d), F32)
    two = jax.ShapeDtypeStruct((2, bsz, tu, d), F32)
    dir_spec = pl.BlockSpec((2, 1, TM, d), lambda b, i: (0, b, i, 0))
    return pl.pallas_call(
        functools.partial(_rwkv_feat_kernel, has_vres, nt),
        out_shape=[one, one, one, two, two, two, one, one],
        grid=(bsz, nt),
        in_specs=[_row_spec(d), prev_spec, next_spec, _mod_spec(d)] + [_full(w.shape) for w in weights] + extra_specs,
        out_specs=[_row_spec(d)] * 3 + [dir_spec] * 3 + [_row_spec(d)] * 2,
        compiler_params=_params("parallel", "parallel"),
        name="rwkv_feat",
    )(x, x, x, mod, *weights, *extra)


def _dot_nt(a, b):
    return lax.dot_general(a, b, (((1,), (1,)), ((), ())), preferred_element_type=F32)


def _wkv_kernel(r_ref, v_ref, kk_ref, lw_ref, k_ref, b_ref, cum_ref, msk_ref, y_ref, st_scr):
    @pl.when(pl.program_id(2) == 0)
    def _():
        st_scr[...] = jnp.zeros_like(st_scr)

    n_rows = r_ref.shape[0]
    n_pairs = r_ref.shape[-1] // LANES
    cum = cum_ref[0]
    at, rt, kt, bt, vs, lam_tot = [], [], [], [], [], []
    for bi in range(n_rows):
        lw = lw_ref[0, bi]
        lw_hi = lw.astype(BF16)
        lw_lo = (lw - lw_hi.astype(F32)).astype(BF16)
        cs = _dot(cum, lw_hi) + _dot(cum, lw_lo)
        lam_inv = jnp.exp(-cs)
        rt.append(r_ref[bi] * jnp.exp(cs))
        at.append(-kk_ref[bi] * jnp.exp(cs - lw))
        kt.append(k_ref[0, bi] * lam_inv)
        bt.append(b_ref[0, bi] * lam_inv)
        vs.append(v_ref[bi])
        lam_tot.append(jnp.exp(jnp.sum(lw, axis=0, keepdims=True)))

    strict = msk_ref[0, 0][:, :WKV_CHUNK] > 0.5
    m_ak = msk_ref[0, 1] > 0.5
    m_y = msk_ref[0, 2] > 0.5
    head_of_lane = lax.broadcasted_iota(jnp.int32, (1, LANES), 1) // RWKV_HEAD
    rowi = lax.broadcasted_iota(jnp.int32, (LANES, LANES), 0)
    coli = lax.broadcasted_iota(jnp.int32, (LANES, LANES), 1)
    blockdiag = (rowi // RWKV_HEAD) == (coli // RWKV_HEAD)

    hpp = LANES // RWKV_HEAD
    n_sq = WKV_CHUNK.bit_length() - 1
    pairs = [(bi, p) for bi in range(n_rows) for p in range(n_pairs)]
    heads = [(q, hh) for q in pairs for hh in range(hpp)]
    sls = {q: slice(q[1] * LANES, (q[1] + 1) * LANES) for q in pairs}
    ar = {q: jnp.concatenate([at[q[0]][:, sls[q]], rt[q[0]][:, sls[q]]], axis=0) for q in pairs}
    bk = {q: jnp.concatenate([bt[q[0]][:, sls[q]], kt[q[0]][:, sls[q]]], axis=0).astype(BF16) for q in pairs}
    vp = {q: vs[q[0]][:, sls[q]] for q in pairs}
    vv = {q: jnp.concatenate([vp[q], vp[q]], axis=0).astype(BF16) for q in pairs}
    st = {q: st_scr[q[0] * n_pairs + q[1]] for q in pairs}
    rhs = {q: jnp.concatenate([bk[q], st[q].astype(BF16)], axis=0) for q in pairs}
    arm = {h: jnp.where(head_of_lane == h[1], ar[h[0]], 0.0).astype(BF16) for h in heads}
    gs = {h: _dot_nt(arm[h], rhs[h[0]]) for h in heads}
    g = {h: gs[h][:, :2 * WKV_CHUNK] for h in heads}
    s0 = {h: gs[h][:, 2 * WKV_CHUNK:] for h in heads}
    u = {h: _dot(jnp.where(m_ak, g[h][:WKV_CHUNK], 0.0).astype(BF16), vv[h[0]]) + s0[h][:WKV_CHUNK]
         for h in heads}
    pk = {h: jnp.where(strict, g[h][:WKV_CHUNK, :WKV_CHUNK], 0.0) for h in heads}
    for it in range(n_sq):
        pkb = {h: pk[h].astype(BF16) for h in heads}
        u = {h: u[h] + _dot(pkb[h], u[h].astype(BF16)) for h in heads}
        if it < n_sq - 1:
            pk = {h: _dot(pkb[h], pkb[h]) for h in heads}
    ys = {h: _dot(jnp.where(m_y, g[h][WKV_CHUNK:], 0.0).astype(BF16),
                  jnp.concatenate([u[h], vp[h[0]]], axis=0).astype(BF16)) + s0[h][WKV_CHUNK:]
          for h in heads}
    for q in pairs:
        y_ref[0, q[0], :, sls[q]] = jnp.where(head_of_lane == 0, ys[(q, 0)], ys[(q, 1)])
        u_pair = jnp.where(head_of_lane == 0, u[(q, 0)], u[(q, 1)])
        uvp = jnp.concatenate([u_pair, vp[q]], axis=0).astype(BF16)
        upd = lax.dot_general(uvp, bk[q], (((0,), (0,)), ((), ())), preferred_element_type=F32)
        st_scr[q[0] * n_pairs + q[1]] = jnp.where(blockdiag, (st[q] + upd) * lam_tot[q[0]][:, sls[q]], 0.0)


def _scan_tables():
    t = np.arange(WKV_CHUNK)
    incl = [t[None, :] <= t[:, None], t[None, :] >= t[:, None]]
    strict = [t[None, :] < t[:, None], t[None, :] > t[:, None]]
    msk = np.zeros((2, 3, WKV_CHUNK, 2 * WKV_CHUNK), np.float32)
    for dd in range(2):
        msk[dd, 0, :, :WKV_CHUNK] = strict[dd]
        msk[dd, 1, :, WKV_CHUNK:] = strict[dd]
        msk[dd, 2, :, :WKV_CHUNK] = incl[dd]
        msk[dd, 2, :, WKV_CHUNK:] = incl[dd]
    return jnp.asarray(np.stack(incl).astype(np.float32), BF16), jnp.asarray(msk, F32)


def _wkv_scan(r, v, kk, lw, k, b, n_ctx):
    bsz, tu, d = r.shape
    nb = tu // WKV_CHUNK
    cb = n_ctx // WKV_CHUNK
    cum, msk = _scan_tables()

    def blk(dd, j):
        back = jnp.where(j < cb, cb - 1 - j, nb - 1 + cb - j)
        return jnp.where(dd == 0, j, back)

    rows = WKV_ROWS if bsz % WKV_ROWS == 0 else 1
    shared = pl.BlockSpec((rows, WKV_CHUNK, d), lambda dd, bb, j: (bb, blk(dd, j), 0))
    per_dir = pl.BlockSpec((1, rows, WKV_CHUNK, d), lambda dd, bb, j: (dd, bb, blk(dd, j), 0))
    return pl.pallas_call(
        _wkv_kernel,
        out_shape=jax.ShapeDtypeStruct((2, bsz, tu, d), F32),
        grid=(2, bsz // rows, nb),
        in_specs=[shared, shared, shared, per_dir, per_dir, per_dir,
                  pl.BlockSpec((1, WKV_CHUNK, WKV_CHUNK), lambda dd, bb, j: (dd, 0, 0)),
                  pl.BlockSpec((1, 3, WKV_CHUNK, 2 * WKV_CHUNK), lambda dd, bb, j: (dd, 0, 0, 0))],
        out_specs=per_dir,
        scratch_shapes=[pltpu.VMEM((rows * (d // LANES), LANES, LANES), F32)],
        compiler_params=_params("arbitrary", "arbitrary", "arbitrary"),
        name="wkv_scan",
    )(r, v, kk, lw, k, b, cum, msk)


def _rwkv_out_kernel(y_ref, g_ref, bv_ref, x_ref, mod_ref, lnw_ref, lnb_ref, wo_ref, hs1_ref, hs2_ref,
                     nf_ref, rw_ref, rb_ref, x_o, hf_o, idx_o, p_o):
    mod = mod_ref[0, 0]
    y = y_ref[0, 0] + y_ref[1, 0]
    inv_n = 1.0 / RWKV_HEAD
    mu = _head_sum(y, hs1_ref, hs2_ref) * inv_n
    dlt = y - mu
    var = _head_sum(dlt * dlt, hs1_ref, hs2_ref) * inv_n
    yn = dlt * lax.rsqrt(var + RWKV_GN_EPS) * lnw_ref[...] + lnb_ref[...]
    o = _bdot((yn + bv_ref[0]) * g_ref[0], wo_ref)
    x2 = x_ref[0] + mod[2:3] * o
    x_o[0] = x2
    _ffn_tail(x2, mod, nf_ref, rw_ref, rb_ref, hf_o, idx_o, p_o)


def _rwkv_out(y, g, bv, x, mod, rp, nf, rw, rb):
    bsz, tu, d = x.shape
    weights = [rp["ln_w"], rp["ln_b"], rp["w_o"], rp["hs1"], rp["hs2"], nf, rw, rb]
    return pl.pallas_call(
        _rwkv_out_kernel,
        out_shape=_tail_out_shapes(bsz, tu, d),
        grid=(bsz, tu // TM),
        in_specs=[pl.BlockSpec((2, 1, TM, d), lambda b, i: (0, b, i, 0))] + [_row_spec(d)] * 3 + [_mod_spec(d)]
        + [_full(w.shape) for w in weights],
        out_specs=_tail_out_specs(d),
        compiler_params=_params("parallel", "parallel"),
        name="rwkv_out",
    )(y, g, bv, x, mod, *weights)


def _dispatch_kernel(pos_ref, h_ref, init_hbm, out_hbm, hbuf, sem):
    del init_hbm
    h = h_ref[...]
    for s in range(SUBLANES):
        hbuf[:, s, :] = h[:, s * LANES:(s + 1) * LANES]

    def issue(jj, carry):
        for kk in range(TOP_K):
            pltpu.make_async_copy(hbuf.at[jj], out_hbm.at[pos_ref[jj * TOP_K + kk]], sem).start()
        return carry

    lax.fori_loop(0, TM, issue, 0)
    for _ in range(TOP_K):
        pltpu.make_async_copy(hbuf, out_hbm.at[pl.ds(0, TM)], sem).wait()


def _dispatch(pos_flat, h2, n_rows):
    n_tok, d = h2.shape
    init = jnp.zeros((n_rows, SUBLANES, LANES), h2.dtype)
    return pl.pallas_call(
        _dispatch_kernel,
        out_shape=jax.ShapeDtypeStruct(init.shape, init.dtype),
        grid=(n_tok // TM,),
        in_specs=[pl.BlockSpec((TM * TOP_K,), lambda i: (i,), memory_space=pltpu.SMEM),
                  pl.BlockSpec((TM, d), lambda i: (i, 0)), pl.BlockSpec(memory_space=pl.ANY)],
        out_specs=pl.BlockSpec(memory_space=pl.ANY),
        scratch_shapes=[pltpu.VMEM((TM, SUBLANES, LANES), h2.dtype), pltpu.SemaphoreType.DMA(())],
        input_output_aliases={2: 0},
        compiler_params=_params("arbitrary"),
        name="moe_dispatch",
    )(pos_flat, h2, init)


def _deinterleave_kernel(w_ref, p_ref, o_ref):
    o_ref[...] = _dot(w_ref[...].astype(BF16), p_ref[...]).astype(BF16)


def _deinterleave_w1(w1):
    f2 = w1.shape[-1]
    rows = int(np.prod(w1.shape[:-1]))
    col = jnp.arange(f2, dtype=jnp.int32)
    src = jnp.where(col < f2 // 2, 2 * col, 2 * (col - f2 // 2) + 1)
    perm = (col[:, None] == src[None, :]).astype(BF16)
    out = pl.pallas_call(
        _deinterleave_kernel,
        out_shape=jax.ShapeDtypeStruct((rows, f2), BF16),
        grid=(rows // TM_FFN,),
        in_specs=[pl.BlockSpec((TM_FFN, f2), lambda i: (i, 0)), pl.BlockSpec((f2, f2), lambda i: (0, 0))],
        out_specs=pl.BlockSpec((TM_FFN, f2), lambda i: (i, 0)),
        compiler_params=_params("parallel"),
        name="w1_deinterleave",
    )(w1.reshape(rows, f2), perm)
    return out.reshape(w1.shape)


def _rows_from_tiles(ref):
    return jnp.concatenate([ref[:, s, :] for s in range(SUBLANES)], axis=1)


def _moe_ffn_kernel(te_ref, nu_ref, hs_ref, w1_ref, b1_ref, w2_ref, b2_ref, ys_ref):
    t = pl.program_id(0)

    @pl.when(t < nu_ref[0])
    def _():
        x = _rows_from_tiles(hs_ref).astype(BF16)
        u = _dot(x, w1_ref[0]) + b1_ref[0]
        f = u.shape[-1] // 2
        glu = jnp.minimum(u[:, :f], SWIGLU_LIMIT)
        lin = jnp.clip(u[:, f:], -SWIGLU_LIMIT, SWIGLU_LIMIT)
        act = glu * _sigmoid(SWIGLU_ALPHA * glu) * (lin + 1.0)
        y = _dot(act.astype(BF16), w2_ref[0]) + b2_ref[0]
        for s in range(SUBLANES):
            ys_ref[:, s, :] = y[:, s * LANES:(s + 1) * LANES]

    @pl.when(t >= nu_ref[0])
    def _():
        ys_ref[...] = jnp.zeros_like(ys_ref)


def _moe_ffn(tile_expert, n_used, hs3, w1, b1, w2, b2):
    n_rows = hs3.shape[0]
    n_tiles = n_rows // TM_FFN
    d = w1.shape[1]
    f2 = w1.shape[2]
    row_map = lambda t, te, nu: (jnp.minimum(t, nu[0] - 1), 0, 0)
    exp_map = lambda t, te, nu: (te[t], 0, 0)
    grid_spec = pltpu.PrefetchScalarGridSpec(
        num_scalar_prefetch=2,
        grid=(n_tiles,),
        in_specs=[pl.BlockSpec((TM_FFN, SUBLANES, LANES), row_map),
                  pl.BlockSpec((1, d, f2), exp_map), pl.BlockSpec((1, 1, f2), exp_map),
                  pl.BlockSpec((1, f2 // 2, d), exp_map), pl.BlockSpec((1, 1, d), exp_map)],
        out_specs=pl.BlockSpec((TM_FFN, SUBLANES, LANES), lambda t, te, nu: (t, 0, 0)),
    )
    return pl.pallas_call(
        _moe_ffn_kernel,
        out_shape=jax.ShapeDtypeStruct(hs3.shape, F32),
        grid_spec=grid_spec,
        compiler_params=_params("arbitrary"),
        name="moe_ffn",
    )(tile_expert, n_used, hs3, w1, b1, w2, b2)


def _combine_kernel(pos_ref, p_ref, ys_hbm, x_ref, mod_ref, x_o, ybuf, abuf, sem):
    def issue(jj, carry):
        for kk in range(TOP_K):
            pltpu.make_async_copy(ys_hbm.at[pos_ref[jj * TOP_K + kk]], ybuf.at[kk * TM + jj], sem).start()
        return carry

    lax.fori_loop(0, TM, issue, 0)
    n = TM * TOP_K
    pltpu.make_async_copy(ys_hbm.at[pl.ds(0, n)], ybuf, sem).wait()

    def mix(jj, carry):
        acc = p_ref[jj * TOP_K] * ybuf[jj]
        for kk in range(1, TOP_K):
            acc = acc + p_ref[jj * TOP_K + kk] * ybuf[kk * TM + jj]
        abuf[jj] = acc
        return carry

    lax.fori_loop(0, TM, mix, 0, unroll=8)
    x_o[0] = x_ref[0] + mod_ref[0, 0][5:6] * _rows_from_tiles(abuf)


def _combine(pos_flat, p_flat, ys3, x, mod):
    bsz, tu, d = x.shape
    nt = tu // TM
    smem_spec = pl.BlockSpec((TM * TOP_K,), lambda b, i: (b * nt + i,), memory_space=pltpu.SMEM)
    return pl.pallas_call(
        _combine_kernel,
        out_shape=jax.ShapeDtypeStruct(x.shape, F32),
        grid=(bsz, nt),
        in_specs=[smem_spec, smem_spec, pl.BlockSpec(memory_space=pl.ANY), _row_spec(d), _mod_spec(d)],
        out_specs=_row_spec(d),
        scratch_shapes=[pltpu.VMEM((TM * TOP_K, SUBLANES, LANES), F32), pltpu.VMEM((TM, SUBLANES, LANES), F32),
                        pltpu.SemaphoreType.DMA(())],
        compiler_params=_params("arbitrary", "arbitrary"),
        name="moe_combine",
    )(pos_flat, p_flat, ys3, x, mod)


def _moe(x, hf, idx_pad, p_pad, mod, w1, b1, w2, b2):
    bsz, tu, d = x.shape
    n_tok = bsz * tu
    n_rows = n_tok * TOP_K + N_EXPERTS * TM_FFN
    n_tiles = n_rows // TM_FFN
    idx = idx_pad.reshape(n_tok, LANES)[:, :TOP_K]
    onehot = jnp.sum((idx[:, :, None] == jnp.arange(N_EXPERTS, dtype=jnp.int32)).astype(jnp.int32), axis=1)
    csum = jnp.cumsum(onehot, axis=0)
    counts = csum[-1]
    padded = ((counts + TM_FFN - 1) // TM_FFN) * TM_FFN
    ends = jnp.cumsum(padded)
    pos = (ends - padded)[idx] + jnp.take_along_axis(csum - onehot, idx, axis=1)
    n_used = (ends[-1] // TM_FFN).astype(jnp.int32)
    tile_start = jnp.arange(n_tiles, dtype=jnp.int32) * TM_FFN
    tile_expert = jnp.sum((tile_start[:, None] >= ends[None, :]).astype(jnp.int32), axis=1)
    last = jnp.take(tile_expert, jnp.maximum(n_used - 1, 0))
    tile_expert = jnp.where(jnp.arange(n_tiles) < n_used, tile_expert, last).astype(jnp.int32)
    pos_flat = pos.reshape(-1).astype(jnp.int32)

    hs3 = _dispatch(pos_flat, hf.reshape(n_tok, d), n_rows)
    ys3 = _moe_ffn(tile_expert, n_used.reshape(1), hs3, w1, b1, w2, b2)
    p_flat = p_pad.reshape(n_tok, LANES)[:, :TOP_K].reshape(-1)
    return _combine(pos_flat, p_flat, ys3, x, mod)


def _rope_tables(n_ctx, n_lat):
    def angles(rot_dim):
        rows = n_lat // GRID_W
        row = jnp.repeat(jnp.arange(rows, dtype=F32), GRID_W)
        col = jnp.tile(jnp.arange(GRID_W, dtype=F32), rows)
        half = rot_dim // 2
        inv_freq = ROPE_THETA ** (-jnp.arange(0, half, 2, dtype=F32) / half)
        return row[:, None] * inv_freq[None, :], col[:, None] * inv_freq[None, :]

    def with_ctx(t, fill):
        return jnp.concatenate([jnp.full((n_ctx, LANES), fill, F32), t], axis=0)

    ar, ac = angles(MLA_ROPE)
    one = jnp.ones((n_lat, 32), F32)
    zero = jnp.zeros((n_lat, 32), F32)
    cm = jnp.concatenate([jnp.cos(ar), jnp.cos(ac), one, jnp.cos(ar), jnp.cos(ac), one], axis=1)
    sm = jnp.concatenate([-jnp.sin(ar), -jnp.sin(ac), zero, jnp.sin(ar), jnp.sin(ac), zero], axis=1)
    ar, ac = angles(GQA_HEAD_DIM)
    cg = jnp.concatenate([jnp.cos(ar), jnp.cos(ac), jnp.cos(ar), jnp.cos(ac)], axis=1)
    sg = jnp.concatenate([-jnp.sin(ar), -jnp.sin(ac), jnp.sin(ar), jnp.sin(ac)], axis=1)
    return with_ctx(cm, 1.0), with_ctx(sm, 0.0), with_ctx(cg, 1.0), with_ctx(sg, 0.0)


_PERM_GQA = np.concatenate([np.arange(0, 32), np.arange(64, 96), np.arange(32, 64), np.arange(96, 128)])
_SRC_MLA = np.zeros(LANES, np.int32)
_MASK_MLA = np.zeros(LANES, np.float32)
_SRC_MLA[0:16], _SRC_MLA[16:32], _SRC_MLA[64:80], _SRC_MLA[80:96] = (
    np.arange(0, 16), np.arange(32, 48), np.arange(16, 32), np.arange(48, 64))
_MASK_MLA[0:32] = 1.0
_MASK_MLA[64:96] = 1.0


def _pad_rope64(w):
    return jnp.take(w, _SRC_MLA, axis=-1) * _MASK_MLA


def _attn_weights(e, w_in, q_norm, w_uq, kv_norm, w_ukv, qn_g, qr_g, kn_g, kr_g, gq_g, gk_g, w_out):
    s1 = MLA_Q_RANK
    s2 = s1 + MLA_KV_RANK
    s3 = s2 + MLA_ROPE
    s4 = s3 + GQA_HEADS * GQA_HEAD_DIM
    s5 = s4 + GQA_KV_HEADS * GQA_HEAD_DIM
    wi = w_in[e]
    d = wi.shape[0]
    gq = wi[:, s3:s4].reshape(d, GQA_HEADS, GQA_HEAD_DIM)[:, :, _PERM_GQA].reshape(d, -1)
    gk = wi[:, s4:s5].reshape(d, GQA_KV_HEADS, GQA_HEAD_DIM)[:, :, _PERM_GQA].reshape(d, -1)
    w_in_p = jnp.concatenate([wi[:, :s2], _pad_rope64(wi[:, s2:s3]), gq, gk, wi[:, s5:]], axis=1)
    uq = w_uq[e].reshape(MLA_Q_RANK, MLA_HEADS, MLA_NOPE + MLA_ROPE)
    uq_p = jnp.concatenate([uq[:, :, :MLA_NOPE], _pad_rope64(uq[:, :, MLA_NOPE:])], axis=-1)
    row = lambda g: g.reshape(1, -1).astype(F32)
    return dict(
        w_in=w_in_p.astype(BF16), q_norm=row(q_norm[e]), w_uq=uq_p.reshape(MLA_Q_RANK, -1).astype(BF16),
        kv_norm=row(kv_norm[e]), w_ukv=w_ukv[e].astype(BF16),
        qn_g=row(qn_g[e]), qr_g=row(_pad_rope64(qr_g[e])), kn_g=row(kn_g[e]), kr_g=row(_pad_rope64(kr_g[e])),
        gq_g=row(gq_g[e][_PERM_GQA]), gk_g=row(gk_g[e][_PERM_GQA]), w_out=w_out[e].astype(BF16))


def _pad_cols(w, n):
    return jnp.pad(w, ((0, 0), (0, n - w.shape[1])))


def _pad_rows(w, n):
    return jnp.pad(w, ((0, n - w.shape[0]), (0, 0)))


def _lora_pair(w1, w2):
    rank = w1.shape[-1]
    down = jnp.concatenate([w1[0], w1[1]], axis=1)
    up = jnp.stack([jnp.pad(w2[0], ((0, rank), (0, 0))), jnp.pad(w2[1], ((rank, 0), (0, 0)))])
    return down.astype(BF16), up.astype(BF16)


def _rwkv_weights(j, mix, w_r, w_k, w_v, w_o, w0, w1, w2, a0, a1, a2, v0, v1, v2, g1, g2, k_k, k_a, r_k, ln_w, ln_b):
    d = w_r.shape[1]
    row = lambda g: g.reshape(1, -1).astype(F32)
    heads = np.arange(d) // RWKV_HEAD
    hs1 = (heads[:, None] == np.arange(LANES)[None, :]).astype(np.float32)
    w1c, w2p = _lora_pair(w1[j], w2[j])
    a1c, a2p = _lora_pair(a1[j], a2[j])
    rp = dict(mix=mix[j], w_r=w_r[j].astype(BF16), w_k=w_k[j].astype(BF16), w_v=w_v[j].astype(BF16),
              w_o=w_o[j].astype(BF16), w0=w0[j], w1=w1c, w2=w2p, a0=a0[j], a1=a1c, a2=a2p,
              g1=g1[j].astype(BF16), g2=g2[j].astype(BF16), k_k=row(k_k[j]), k_a=row(k_a[j]), r_k=row(r_k[j]),
              ln_w=row(ln_w[j]), ln_b=row(ln_b[j]),
              hs1=jnp.asarray(hs1, BF16), hs2=jnp.asarray(hs1.T, BF16))
    if j > 0:
        rp.update(v0=row(v0[j - 1]), v1=_pad_cols(v1[j - 1], LANES).astype(BF16),
                  v2=_pad_rows(v2[j - 1], LANES).astype(BF16))
    return rp


def kernel(x, c, ctx, c_ctx, ada_w, ada_b, norm_mix, norm_ffn, attn_w_in, mla_q_norm, mla_w_uq, mla_kv_norm, mla_w_ukv, mla_qn_g, mla_qr_g, mla_kn_g, mla_kr_g, gqa_q_g, gqa_k_g, attn_w_out, rwkv_mix, rwkv_w_r, rwkv_w_k, rwkv_w_v, rwkv_w_o, rwkv_w0, rwkv_w1, rwkv_w2, rwkv_a0, rwkv_a1, rwkv_a2, rwkv_v0, rwkv_v1, rwkv_v2, rwkv_g1, rwkv_g2, rwkv_k_k, rwkv_k_a, rwkv_r_k, rwkv_ln_w, rwkv_ln_b, moe_router_w, moe_router_b, moe_w1, moe_b1, moe_w2, moe_b2):
    bsz, n_lat, d = x.shape
    n_ctx = ctx.shape[1]
    depth = ada_w.shape[0]
    tu = n_ctx + n_lat
    assert n_ctx % TM == 0 and n_lat % TM == 0 and n_ctx % WKV_CHUNK == 0 and d == SUBLANES * LANES
    assert (bsz * tu * TOP_K) % TM_FFN == 0

    rows = -(-(bsz + 1) // SUBLANES) * SUBLANES
    cond = jnp.zeros((rows, d), F32).at[:bsz].set(c).at[bsz].set(c_ctx)
    mods = _adaln(cond, ada_w, ada_b)
    mods_lat = mods[:, :bsz].reshape(depth, bsz, 1, N_MOD, d)
    mods_ctx = jnp.broadcast_to(mods[:, bsz].reshape(depth, 1, 1, N_MOD, d), mods_lat.shape)
    mods = jnp.concatenate([mods_ctx, mods_lat], axis=2)

    tabs = _rope_tables(n_ctx, n_lat)
    xs = jnp.concatenate([ctx, x], axis=1)

    n_e, _, f2 = moe_w1.shape[1:]
    rw_all = jnp.pad(moe_router_w, ((0, 0), (0, 0), (0, LANES - n_e)))
    rb_all = jnp.pad(moe_router_b, ((0, 0), (0, LANES - n_e)), constant_values=-1e30).reshape(depth, 1, LANES)

    w1_all = _deinterleave_w1(moe_w1)

    v_first = None
    for i in range(depth):
        mod = mods[i]
        nm = norm_mix[i].reshape(1, d)
        nf = norm_ffn[i].reshape(1, d)
        rw, rb = rw_all[i], rb_all[i]
        if i % 2 == 0:
            aw = _attn_weights(i // 2, attn_w_in, mla_q_norm, mla_w_uq, mla_kv_norm, mla_w_ukv, mla_qn_g,
                               mla_qr_g, mla_kn_g, mla_kr_g, gqa_q_g, gqa_k_g, attn_w_out)
            qm, km, vm, qg, kg, vg = _attn_proj(xs, mod, nm, aw, tabs)
            om = _attention(qm, km, vm, n_ctx)
            og = _attention(qg, kg, vg, n_ctx)
            xs, hf, idx_pad, p_pad = _attn_out(om, og, xs, mod, aw["w_out"], nf, rw, rb)
        else:
            j = i // 2
            rp = _rwkv_weights(j, rwkv_mix, rwkv_w_r, rwkv_w_k, rwkv_w_v, rwkv_w_o, rwkv_w0, rwkv_w1, rwkv_w2,
                               rwkv_a0, rwkv_a1, rwkv_a2, rwkv_v0, rwkv_v1, rwkv_v2, rwkv_g1, rwkv_g2,
                               rwkv_k_k, rwkv_k_a, rwkv_r_k, rwkv_ln_w, rwkv_ln_b)
            r, v, kk, w, k, b, g, bv = _rwkv_feat(xs, mod, nm, rp, v_first if j > 0 else None)
            if j == 0:
                v_first = v
            y = _wkv_scan(r, v, kk, w, k, b, n_ctx)
            xs, hf, idx_pad, p_pad = _rwkv_out(y, g, bv, xs, mod, rp, nf, rw, rb)
        w1 = w1_all[i]
        b1 = jnp.concatenate([moe_b1[i][:, 0::2], moe_b1[i][:, 1::2]], axis=-1).reshape(n_e, 1, f2)
        w2 = moe_w2[i].astype(BF16)
        b2 = moe_b2[i].reshape(n_e, 1, d)
        xs = _moe(xs, hf, idx_pad, p_pad, mod, w1, b1, w2, b2)
    return xs[:, n_ctx:]
```

```python
import functools

import jax
import jax.numpy as jnp
import numpy as np
from jax import lax
from jax.experimental import pallas as pl
from jax.experimental.pallas import tpu as pltpu

F32 = jnp.float32
BF16 = jnp.bfloat16
HIGHEST = lax.Precision.HIGHEST

SUBLANES = 8
LANES = 128
VMEM_LIMIT = 56 * 1024 * 1024

GRID_W = 64
ROPE_THETA = 10000.0
NORM_EPS = 1e-6
N_MOD = 6

MLA_HEADS = 4
MLA_Q_RANK = 384
MLA_KV_RANK = 256
MLA_NOPE = 128
MLA_ROPE = 64
MLA_V = 128
MLA_SCALE = (MLA_NOPE + MLA_ROPE) ** -0.5
GQA_HEADS = 4
GQA_KV_HEADS = 2
GQA_HEAD_DIM = 128
GQA_SCALE = GQA_HEAD_DIM ** -0.5
LOG2E = float(np.log2(np.e))

RWKV_HEAD = 64
RWKV_GN_EPS = 64e-5
N_MIX = 6
DECAY_SCALE = float(np.exp(-0.5))

N_EXPERTS = 32
TOP_K = 4
SWIGLU_ALPHA = 1.702
SWIGLU_LIMIT = 7.0

TM = 256
TM_FFN = 512
WKV_CHUNK = 64
WKV_ROWS = 2


def _params(*sem):
    return pltpu.CompilerParams(dimension_semantics=sem, vmem_limit_bytes=VMEM_LIMIT)


def _dot(a, b):
    return jnp.dot(a, b, preferred_element_type=F32)


def _bdot(a, b_ref):
    return jnp.dot(a.astype(BF16), b_ref[...], preferred_element_type=F32)


def _rms(x, g, n):
    ms = jnp.sum(x * x, axis=-1, keepdims=True) * (1.0 / n)
    return x * lax.rsqrt(ms + NORM_EPS) * g


def _sigmoid(x):
    return 1.0 / (1.0 + jnp.exp(-x))


def _rope(x, cos, sin):
    return x * cos + pltpu.roll(x, 64, axis=1) * sin


def _adaln_kernel(c_ref, w_ref, b_ref, o_ref):
    c = c_ref[...]
    s = c * _sigmoid(c)
    o_ref[0] = jnp.dot(s, w_ref[0], precision=HIGHEST, preferred_element_type=F32) + b_ref[0]


def _adaln(cond, ada_w, ada_b):
    depth, d, nd = ada_w.shape
    rows = cond.shape[0]
    return pl.pallas_call(
        _adaln_kernel,
        out_shape=jax.ShapeDtypeStruct((depth, rows, nd), F32),
        grid=(depth, nd // d),
        in_specs=[pl.BlockSpec((rows, d), lambda l, n: (0, 0)),
                  pl.BlockSpec((1, d, d), lambda l, n: (l, 0, n)),
                  pl.BlockSpec((1, 1, d), lambda l, n: (l, 0, n))],
        out_specs=pl.BlockSpec((1, rows, d), lambda l, n: (l, 0, n)),
        compiler_params=_params("arbitrary", "arbitrary"),
        name="adaln",
    )(cond, ada_w, ada_b.reshape(depth, 1, nd))


def _ffn_tail(x2, mod, nf_ref, rw_ref, rb_ref, hf_o, idx_o, p_o):
    d = x2.shape[-1]
    hf = _rms(x2, nf_ref[...], d) * (1.0 + mod[4:5]) + mod[3:4]
    hf_o[0] = hf
    logits = jnp.dot(hf, rw_ref[...], precision=HIGHEST, preferred_element_type=F32) + rb_ref[...]
    lane = lax.broadcasted_iota(jnp.int32, logits.shape, 1).astype(F32)
    vals, idxs = [], []
    l = logits
    for _ in range(TOP_K):
        m = jnp.max(l, axis=-1, keepdims=True)
        ik = jnp.min(jnp.where(l == m, lane, float(LANES)), axis=-1, keepdims=True)
        vals.append(m)
        idxs.append(ik)
        l = jnp.where(lane == ik, -jnp.inf, l)
    es = [jnp.exp(v - vals[0]) for v in vals]
    inv = 1.0 / (es[0] + es[1] + es[2] + es[3])
    idx_acc = jnp.zeros_like(logits)
    p_acc = jnp.zeros_like(logits)
    for k in range(TOP_K):
        idx_acc = jnp.where(lane == float(k), idxs[k], idx_acc)
        p_acc = jnp.where(lane == float(k), es[k] * inv, p_acc)
    idx_o[0] = idx_acc.astype(jnp.int32)
    p_o[0] = p_acc


def _attn_proj_kernel(x_ref, mod_ref, nm_ref, win_ref, qn_ref, wuq_ref, kvn_ref, wukv_ref,
                      qng_ref, qrg_ref, kng_ref, krg_ref, gqg_ref, gkg_ref,
                      cm_ref, sm_ref, cg_ref, sg_ref,
                      qm_o, km_o, vm_o, qg_o, kg_o, vg_o):
    x = x_ref[0]
    mod = mod_ref[0, 0]
    d = x.shape[-1]
    h = _rms(x, nm_ref[...], d) * (1.0 + mod[1:2]) + mod[0:1]
    z = _bdot(h, win_ref)
    o1 = MLA_Q_RANK
    o2 = o1 + MLA_KV_RANK
    o3 = o2 + LANES
    o4 = o3 + GQA_HEADS * GQA_HEAD_DIM
    o5 = o4 + GQA_KV_HEADS * GQA_HEAD_DIM
    q = _bdot(_rms(z[:, :o1], qn_ref[...], MLA_Q_RANK), wuq_ref)
    kv = _bdot(_rms(z[:, o1:o2], kvn_ref[...], MLA_KV_RANK), wukv_ref)
    cm, sm, cg, sg = cm_ref[...], sm_ref[...], cg_ref[...], sg_ref[...]
    kr = _rope(_rms(z[:, o2:o3], krg_ref[...], MLA_ROPE), cm, sm)
    for hh in range(MLA_HEADS):
        b0 = hh * 2 * LANES
        qn = _rms(q[:, b0:b0 + LANES], qng_ref[...], MLA_NOPE)
        qr = _rope(_rms(q[:, b0 + LANES:b0 + 2 * LANES], qrg_ref[...], MLA_ROPE), cm, sm)
        qm_o[0, hh] = (jnp.concatenate([qn, qr], axis=1) * (MLA_SCALE * LOG2E)).astype(BF16)
        kn = _rms(kv[:, b0:b0 + LANES], kng_ref[...], MLA_NOPE)
        km_o[0, hh] = jnp.concatenate([kn, kr], axis=1).astype(BF16)
        vm_o[0, hh] = kv[:, b0 + LANES:b0 + 2 * LANES].astype(BF16)
    for hh in range(GQA_HEADS):
        g = z[:, o3 + hh * LANES:o3 + (hh + 1) * LANES]
        qg_o[0, hh] = (_rope(_rms(g, gqg_ref[...], GQA_HEAD_DIM), cg, sg) * (GQA_SCALE * LOG2E)).astype(BF16)
    for hh in range(GQA_KV_HEADS):
        g = z[:, o4 + hh * LANES:o4 + (hh + 1) * LANES]
        kg_o[0, hh] = _rope(_rms(g, gkg_ref[...], GQA_HEAD_DIM), cg, sg).astype(BF16)
        vg_o[0, hh] = z[:, o5 + hh * LANES:o5 + (hh + 1) * LANES].astype(BF16)


def _full(shape):
    nd = len(shape)
    return pl.BlockSpec(shape, lambda b, i: (0,) * nd)


def _mod_spec(d):
    return pl.BlockSpec((1, 1, N_MOD, d), lambda b, i: (b, jnp.minimum(i, 1), 0, 0))


def _row_spec(d):
    return pl.BlockSpec((1, TM, d), lambda b, i: (b, i, 0))


def _attn_proj(x, mod, nm, aw, tabs):
    bsz, tu, d = x.shape
    nt = tu // TM
    head_spec = lambda nh, w: pl.BlockSpec((1, nh, TM, w), lambda b, i: (b, 0, i, 0))
    tab_spec = pl.BlockSpec((TM, LANES), lambda b, i: (i, 0))
    weights = [nm, aw["w_in"], aw["q_norm"], aw["w_uq"], aw["kv_norm"], aw["w_ukv"],
               aw["qn_g"], aw["qr_g"], aw["kn_g"], aw["kr_g"], aw["gq_g"], aw["gk_g"]]
    out_shape = [
        jax.ShapeDtypeStruct((bsz, MLA_HEADS, tu, 2 * LANES), BF16),
        jax.ShapeDtypeStruct((bsz, MLA_HEADS, tu, 2 * LANES), BF16),
        jax.ShapeDtypeStruct((bsz, MLA_HEADS, tu, LANES), BF16),
        jax.ShapeDtypeStruct((bsz, GQA_HEADS, tu, LANES), BF16),
        jax.ShapeDtypeStruct((bsz, GQA_KV_HEADS, tu, LANES), BF16),
        jax.ShapeDtypeStruct((bsz, GQA_KV_HEADS, tu, LANES), BF16),
    ]
    return pl.pallas_call(
        _attn_proj_kernel,
        out_shape=out_shape,
        grid=(bsz, nt),
        in_specs=[_row_spec(d), _mod_spec(d)] + [_full(w.shape) for w in weights] + [tab_spec] * 4,
        out_specs=[head_spec(MLA_HEADS, 2 * LANES), head_spec(MLA_HEADS, 2 * LANES),
                   head_spec(MLA_HEADS, LANES), head_spec(GQA_HEADS, LANES),
                   head_spec(GQA_KV_HEADS, LANES), head_spec(GQA_KV_HEADS, LANES)],
        compiler_params=_params("parallel", "parallel"),
        name="attn_proj",
    )(x, mod, *weights, *tabs)


def _attn_kernel(n_ctx, q_ref, k_ref, v_ref, o_ref):
    q = q_ref[0, 0]

    def attend(k, v):
        s = lax.dot_general(q, k, (((1,), (1,)), ((), ())), preferred_element_type=F32)
        m = jnp.max(s, axis=-1, keepdims=True)
        p = jnp.exp2(s - m)
        l = jnp.sum(p, axis=-1, keepdims=True)
        o = _dot(p.astype(BF16), v)
        o_ref[0] = (o * (1.0 / l)).astype(o_ref.dtype)

    is_ctx = pl.program_id(2) == 0

    @pl.when(is_ctx)
    def _():
        attend(k_ref[0, 0, :n_ctx], v_ref[0, 0, :n_ctx])

    @pl.when(jnp.logical_not(is_ctx))
    def _():
        attend(k_ref[0, 0], v_ref[0, 0])


def _attention(q, k, v, n_ctx):
    bsz, hq, tu, dk = q.shape
    hk = k.shape[1]
    grp = hq // hk
    dv = v.shape[-1]
    return pl.pallas_call(
        functools.partial(_attn_kernel, n_ctx),
        out_shape=jax.ShapeDtypeStruct((bsz, tu, hq * dv), BF16),
        grid=(bsz, hq, tu // TM),
        in_specs=[pl.BlockSpec((1, 1, TM, dk), lambda b, h, i: (b, h, i, 0)),
                  pl.BlockSpec((1, 1, tu, dk), lambda b, h, i: (b, h // grp, 0, 0)),
                  pl.BlockSpec((1, 1, tu, dv), lambda b, h, i: (b, h // grp, 0, 0))],
        out_specs=pl.BlockSpec((1, TM, dv), lambda b, h, i: (b, i, h)),
        compiler_params=_params("parallel", "parallel", "arbitrary"),
        name="attention",
    )(q, k, v)


def _attn_out_kernel(om_ref, og_ref, x_ref, mod_ref, wo_ref, nf_ref, rw_ref, rb_ref,
                     x_o, hf_o, idx_o, p_o):
    mod = mod_ref[0, 0]
    nm = om_ref.shape[-1]
    o = _dot(om_ref[0], wo_ref[:nm]) + _dot(og_ref[0], wo_ref[nm:])
    x2 = x_ref[0] + mod[2:3] * o
    x_o[0] = x2
    _ffn_tail(x2, mod, nf_ref, rw_ref, rb_ref, hf_o, idx_o, p_o)


def _tail_out_shapes(bsz, tu, d):
    return [jax.ShapeDtypeStruct((bsz, tu, d), F32), jax.ShapeDtypeStruct((bsz, tu, d), F32),
            jax.ShapeDtypeStruct((bsz, tu, LANES), jnp.int32), jax.ShapeDtypeStruct((bsz, tu, LANES), F32)]


def _tail_out_specs(d):
    return [_row_spec(d), _row_spec(d), _row_spec(LANES), _row_spec(LANES)]


def _attn_out(om, og, x, mod, wo, nf, rw, rb):
    bsz, tu, d = x.shape
    return pl.pallas_call(
        _attn_out_kernel,
        out_shape=_tail_out_shapes(bsz, tu, d),
        grid=(bsz, tu // TM),
        in_specs=[_row_spec(om.shape[-1]), _row_spec(og.shape[-1]), _row_spec(d), _mod_spec(d),
                  _full(wo.shape), _full(nf.shape), _full(rw.shape), _full(rb.shape)],
        out_specs=_tail_out_specs(d),
        compiler_params=_params("parallel", "parallel"),
        name="attn_out",
    )(om, og, x, mod, wo, nf, rw, rb)


def _head_sum(x, hs1_ref, hs2_ref):
    hi = x.astype(BF16)
    lo = (x - hi.astype(F32)).astype(BF16)
    s = _dot(hi, hs1_ref[...]) + _dot(lo, hs1_ref[...])
    shi = s.astype(BF16)
    slo = (s - shi.astype(F32)).astype(BF16)
    return _dot(shi, hs2_ref[...]) + _dot(slo, hs2_ref[...])


def _rwkv_feat_kernel(has_vres, nt, x_ref, xp_ref, xn_ref, mod_ref, nm_ref, mix_ref,
                      wr_ref, wk_ref, wv_ref, w0_ref, w1_ref, w2_ref, a0_ref, a1_ref, a2_ref,
                      g1_ref, g2_ref, kk_ref, ka_ref, rk_ref, hs1_ref, hs2_ref, *rest):
    if has_vres:
        v0_ref, v1_ref, v2_ref, vf_ref = rest[:4]
        rest = rest[4:]
    r_o, v_o, kk_o, w_o, k_o, b_o, g_o, bv_o = rest
    i = pl.program_id(1)
    mod = mod_ref[0, 0]
    d = x_ref.shape[-1]
    nm = nm_ref[...]

    def hmod(xx):
        return _rms(xx, nm, d) * (1.0 + mod[1:2]) + mod[0:1]

    h = hmod(x_ref[0])
    has_prev = (i >= 2).astype(F32)
    has_next = jnp.logical_and(i >= 1, i < nt - 1).astype(F32)
    prev_row = hmod(xp_ref[0])[SUBLANES - 1:SUBLANES] * has_prev
    next_row = hmod(xn_ref[0])[0:1] * has_next
    row = lax.broadcasted_iota(jnp.int32, h.shape, 0)
    hp = jnp.where(row == 0, prev_row, pltpu.roll(h, 1, axis=0))
    hn = jnp.where(row == TM - 1, next_row, pltpu.roll(h, TM - 1, axis=0))
    xx = 0.5 * (hp + hn) - h
    xr, xw, xk, xv, xa, xg = (h + xx * mix_ref[m:m + 1] for m in range(N_MIX))

    r = _bdot(xr, wr_ref)
    k = _bdot(xk, wk_ref)
    v = _bdot(xv, wv_ref)
    if has_vres:
        gate = _sigmoid(v0_ref[...] + _bdot(_bdot(xv, v1_ref), v2_ref))
        v = v + (vf_ref[0] - v) * gate
    kk = k * kk_ref[...]
    kk = kk * lax.rsqrt(jnp.maximum(_head_sum(kk * kk, hs1_ref, hs2_ref), 1e-24))
    tw = jnp.tanh(_bdot(xw, w1_ref)).astype(BF16)
    ta = _bdot(xa, a1_ref).astype(BF16)
    ksum = jnp.zeros_like(k)
    for dd in range(2):
        z = w0_ref[dd:dd + 1] + _dot(tw, w2_ref[dd])
        w_o[dd, 0] = -DECAY_SCALE * _sigmoid(z)
        a = _sigmoid(a0_ref[dd:dd + 1] + _dot(ta, a2_ref[dd]))
        kd = k * (1.0 + (a - 1.0) * ka_ref[...])
        k_o[dd, 0] = kd
        b_o[dd, 0] = kk * a
        ksum = ksum + kd
    g_o[0] = _bdot(_sigmoid(_bdot(xg, g1_ref)), g2_ref)
    bv_o[0] = _head_sum(r * ksum * rk_ref[...], hs1_ref, hs2_ref) * v
    r_o[0] = r
    v_o[0] = v
    kk_o[0] = kk


def _rwkv_feat(x, mod, nm, rp, v_first):
    bsz, tu, d = x.shape
    nt = tu // TM
    has_vres = v_first is not None
    nblk8 = tu // SUBLANES
    per8 = TM // SUBLANES
    prev_spec = pl.BlockSpec((1, SUBLANES, d), lambda b, i: (b, jnp.maximum(i * per8 - 1, 0), 0))
    next_spec = pl.BlockSpec((1, SUBLANES, d), lambda b, i: (b, jnp.minimum((i + 1) * per8, nblk8 - 1), 0))
    weights = [nm, rp["mix"], rp["w_r"], rp["w_k"], rp["w_v"], rp["w0"], rp["w1"], rp["w2"],
               rp["a0"], rp["a1"], rp["a2"], rp["g1"], rp["g2"], rp["k_k"], rp["k_a"], rp["r_k"],
               rp["hs1"], rp["hs2"]]
    extra, extra_specs = [], []
    if has_vres:
        extra = [rp["v0"], rp["v1"], rp["v2"], v_first]
        extra_specs = [_full(rp["v0"].shape), _full(rp["v1"].shape), _full(rp["v2"].shape), _row_spec(d)]
    one = jax.ShapeDtypeStruct((bsz, tu, d), F32)
    two = jax.ShapeDtypeStruct((2, bsz, tu, d), F32)
    dir_spec = pl.BlockSpec((2, 1, TM, d), lambda b, i: (0, b, i, 0))
    return pl.pallas_call(
        functools.partial(_rwkv_feat_kernel, has_vres, nt),
        out_shape=[one, one, one, two, two, two, one, one],
        grid=(bsz, nt),
        in_specs=[_row_spec(d), prev_spec, next_spec, _mod_spec(d)] + [_full(w.shape) for w in weights] + extra_specs,
        out_specs=[_row_spec(d)] * 3 + [dir_spec] * 3 + [_row_spec(d)] * 2,
        compiler_params=_params("parallel", "parallel"),
        name="rwkv_feat",
    )(x, x, x, mod, *weights, *extra)


def _dot_nt(a, b):
    return lax.dot_general(a, b, (((1,), (1,)), ((), ())), preferred_element_type=F32)


def _wkv_kernel(r_ref, v_ref, kk_ref, lw_ref, k_ref, b_ref, cum_ref, msk_ref, y_ref, st_scr):
    @pl.when(pl.program_id(2) == 0)
    def _():
        st_scr[...] = jnp.zeros_like(st_scr)

    n_rows = r_ref.shape[0]
    n_pairs = r_ref.shape[-1] // LANES
    cum = cum_ref[0]
    at, rt, kt, bt, vs, lam_tot = [], [], [], [], [], []
    for bi in range(n_rows):
        lw = lw_ref[0, bi]
        lw_hi = lw.astype(BF16)
        lw_lo = (lw - lw_hi.astype(F32)).astype(BF16)
        cs = _dot(cum, lw_hi) + _dot(cum, lw_lo)
        lam_inv = jnp.exp(-cs)
        rt.append(r_ref[bi] * jnp.exp(cs))
        at.append(-kk_ref[bi] * jnp.exp(cs - lw))
        kt.append(k_ref[0, bi] * lam_inv)
        bt.append(b_ref[0, bi] * lam_inv)
        vs.append(v_ref[bi])
        lam_tot.append(jnp.exp(jnp.sum(lw, axis=0, keepdims=True)))

    strict = msk_ref[0, 0][:, :WKV_CHUNK] > 0.5
    m_ak = msk_ref[0, 1] > 0.5
    m_y = msk_ref[0, 2] > 0.5
    head_of_lane = lax.broadcasted_iota(jnp.int32, (1, LANES), 1) // RWKV_HEAD
    rowi = lax.broadcasted_iota(jnp.int32, (LANES, LANES), 0)
    coli = lax.broadcasted_iota(jnp.int32, (LANES, LANES), 1)
    blockdiag = (rowi // RWKV_HEAD) == (coli // RWKV_HEAD)

    hpp = LANES // RWKV_HEAD
    n_sq = WKV_CHUNK.bit_length() - 1
    pairs = [(bi, p) for bi in range(n_rows) for p in range(n_pairs)]
    heads = [(q, hh) for q in pairs for hh in range(hpp)]
    sls = {q: slice(q[1] * LANES, (q[1] + 1) * LANES) for q in pairs}
    ar = {q: jnp.concatenate([at[q[0]][:, sls[q]], rt[q[0]][:, sls[q]]], axis=0) for q in pairs}
    bk = {q: jnp.concatenate([bt[q[0]][:, sls[q]], kt[q[0]][:, sls[q]]], axis=0).astype(BF16) for q in pairs}
    vp = {q: vs[q[0]][:, sls[q]] for q in pairs}
    vv = {q: jnp.concatenate([vp[q], vp[q]], axis=0).astype(BF16) for q in pairs}
    st = {q: st_scr[q[0] * n_pairs + q[1]] for q in pairs}
    rhs = {q: jnp.concatenate([bk[q], st[q].astype(BF16)], axis=0) for q in pairs}
    arm = {h: jnp.where(head_of_lane == h[1], ar[h[0]], 0.0).astype(BF16) for h in heads}
    gs = {h: _dot_nt(arm[h], rhs[h[0]]) for h in heads}
    g = {h: gs[h][:, :2 * WKV_CHUNK] for h in heads}
    s0 = {h: gs[h][:, 2 * WKV_CHUNK:] for h in heads}
    u = {h: _dot(jnp.where(m_ak, g[h][:WKV_CHUNK], 0.0).astype(BF16), vv[h[0]]) + s0[h][:WKV_CHUNK]
         for h in heads}
    pk = {h: jnp.where(strict, g[h][:WKV_CHUNK, :WKV_CHUNK], 0.0) for h in heads}
    for it in range(n_sq):
        pkb = {h: pk[h].astype(BF16) for h in heads}
        u = {h: u[h] + _dot(pkb[h], u[h].astype(BF16)) for h in heads}
        if it < n_sq - 1:
            pk = {h: _dot(pkb[h], pkb[h]) for h in heads}
    ys = {h: _dot(jnp.where(m_y, g[h][WKV_CHUNK:], 0.0).astype(BF16),
                  jnp.concatenate([u[h], vp[h[0]]], axis=0).astype(BF16)) + s0[h][WKV_CHUNK:]
          for h in heads}
    for q in pairs:
        y_ref[0, q[0], :, sls[q]] = jnp.where(head_of_lane == 0, ys[(q, 0)], ys[(q, 1)])
        u_pair = jnp.where(head_of_lane == 0, u[(q, 0)], u[(q, 1)])
        uvp = jnp.concatenate([u_pair, vp[q]], axis=0).astype(BF16)
        upd = lax.dot_general(uvp, bk[q], (((0,), (0,)), ((), ())), preferred_element_type=F32)
        st_scr[q[0] * n_pairs + q[1]] = jnp.where(blockdiag, (st[q] + upd) * lam_tot[q[0]][:, sls[q]], 0.0)


def _scan_tables():
    t = np.arange(WKV_CHUNK)
    incl = [t[None, :] <= t[:, None], t[None, :] >= t[:, None]]
    strict = [t[None, :] < t[:, None], t[None, :] > t[:, None]]
    msk = np.zeros((2, 3, WKV_CHUNK, 2 * WKV_CHUNK), np.float32)
    for dd in range(2):
        msk[dd, 0, :, :WKV_CHUNK] = strict[dd]
        msk[dd, 1, :, WKV_CHUNK:] = strict[dd]
        msk[dd, 2, :, :WKV_CHUNK] = incl[dd]
        msk[dd, 2, :, WKV_CHUNK:] = incl[dd]
    return jnp.asarray(np.stack(incl).astype(np.float32), BF16), jnp.asarray(msk, F32)


def _wkv_scan(r, v, kk, lw, k, b, n_ctx):
    bsz, tu, d = r.shape
    nb = tu // WKV_CHUNK
    cb = n_ctx // WKV_CHUNK
    cum, msk = _scan_tables()

    def blk(dd, j):
        back = jnp.where(j < cb, cb - 1 - j, nb - 1 + cb - j)
        return jnp.where(dd == 0, j, back)

    rows = WKV_ROWS if bsz % WKV_ROWS == 0 else 1
    shared = pl.BlockSpec((rows, WKV_CHUNK, d), lambda dd, bb, j: (bb, blk(dd, j), 0))
    per_dir = pl.BlockSpec((1, rows, WKV_CHUNK, d), lambda dd, bb, j: (dd, bb, blk(dd, j), 0))
    return pl.pallas_call(
        _wkv_kernel,
        out_shape=jax.ShapeDtypeStruct((2, bsz, tu, d), F32),
        grid=(2, bsz // rows, nb),
        in_specs=[shared, shared, shared, per_dir, per_dir, per_dir,
                  pl.BlockSpec((1, WKV_CHUNK, WKV_CHUNK), lambda dd, bb, j: (dd, 0, 0)),
                  pl.BlockSpec((1, 3, WKV_CHUNK, 2 * WKV_CHUNK), lambda dd, bb, j: (dd, 0, 0, 0))],
        out_specs=per_dir,
        scratch_shapes=[pltpu.VMEM((rows * (d // LANES), LANES, LANES), F32)],
        compiler_params=_params("arbitrary", "arbitrary", "arbitrary"),
        name="wkv_scan",
    )(r, v, kk, lw, k, b, cum, msk)


def _rwkv_out_kernel(y_ref, g_ref, bv_ref, x_ref, mod_ref, lnw_ref, lnb_ref, wo_ref, hs1_ref, hs2_ref,
                     nf_ref, rw_ref, rb_ref, x_o, hf_o, idx_o, p_o):
    mod = mod_ref[0, 0]
    y = y_ref[0, 0] + y_ref[1, 0]
    inv_n = 1.0 / RWKV_HEAD
    mu = _head_sum(y, hs1_ref, hs2_ref) * inv_n
    dlt = y - mu
    var = _head_sum(dlt * dlt, hs1_ref, hs2_ref) * inv_n
    yn = dlt * lax.rsqrt(var + RWKV_GN_EPS) * lnw_ref[...] + lnb_ref[...]
    o = _bdot((yn + bv_ref[0]) * g_ref[0], wo_ref)
    x2 = x_ref[0] + mod[2:3] * o
    x_o[0] = x2
    _ffn_tail(x2, mod, nf_ref, rw_ref, rb_ref, hf_o, idx_o, p_o)


def _rwkv_out(y, g, bv, x, mod, rp, nf, rw, rb):
    bsz, tu, d = x.shape
    weights = [rp["ln_w"], rp["ln_b"], rp["w_o"], rp["hs1"], rp["hs2"], nf, rw, rb]
    return pl.pallas_call(
        _rwkv_out_kernel,
        out_shape=_tail_out_shapes(bsz, tu, d),
        grid=(bsz, tu // TM),
        in_specs=[pl.BlockSpec((2, 1, TM, d), lambda b, i: (0, b, i, 0))] + [_row_spec(d)] * 3 + [_mod_spec(d)]
        + [_full(w.shape) for w in weights],
        out_specs=_tail_out_specs(d),
        compiler_params=_params("parallel", "parallel"),
        name="rwkv_out",
    )(y, g, bv, x, mod, *weights)


def _dispatch_kernel(pos_ref, h_ref, init_hbm, out_hbm, hbuf, sem):
    del init_hbm
    h = h_ref[...]
    for s in range(SUBLANES):
        hbuf[:, s, :] = h[:, s * LANES:(s + 1) * LANES]

    def issue(jj, carry):
        for kk in range(TOP_K):
            pltpu.make_async_copy(hbuf.at[jj], out_hbm.at[pos_ref[jj * TOP_K + kk]], sem).start()
        return carry

    lax.fori_loop(0, TM, issue, 0)
    for _ in range(TOP_K):
        pltpu.make_async_copy(hbuf, out_hbm.at[pl.ds(0, TM)], sem).wait()


def _dispatch(pos_flat, h2, n_rows):
    n_tok, d = h2.shape
    init = jnp.zeros((n_rows, SUBLANES, LANES), h2.dtype)
    return pl.pallas_call(
        _dispatch_kernel,
        out_shape=jax.ShapeDtypeStruct(init.shape, init.dtype),
        grid=(n_tok // TM,),
        in_specs=[pl.BlockSpec((TM * TOP_K,), lambda i: (i,), memory_space=pltpu.SMEM),
                  pl.BlockSpec((TM, d), lambda i: (i, 0)), pl.BlockSpec(memory_space=pl.ANY)],
        out_specs=pl.BlockSpec(memory_space=pl.ANY),
        scratch_shapes=[pltpu.VMEM((TM, SUBLANES, LANES), h2.dtype), pltpu.SemaphoreType.DMA(())],
        input_output_aliases={2: 0},
        compiler_params=_params("arbitrary"),
        name="moe_dispatch",
    )(pos_flat, h2, init)


def _deinterleave_kernel(w_ref, p_ref, o_ref):
    half = o_ref.shape[-1] // 2
    for j in range(half // LANES):
        blk = _dot(w_ref[:, 2 * LANES * j:2 * LANES * (j + 1)].astype(BF16), p_ref[...]).astype(BF16)
        o_ref[:, LANES * j:LANES * (j + 1)] = blk[:, :LANES]
        o_ref[:, half + LANES * j:half + LANES * (j + 1)] = blk[:, LANES:]


def _deinterleave_w1(w1):
    f2 = w1.shape[-1]
    rows = int(np.prod(w1.shape[:-1]))
    col = jnp.arange(2 * LANES, dtype=jnp.int32)
    src = jnp.where(col < LANES, 2 * col, 2 * (col - LANES) + 1)
    perm = (col[:, None] == src[None, :]).astype(BF16)
    out = pl.pallas_call(
        _deinterleave_kernel,
        out_shape=jax.ShapeDtypeStruct((rows, f2), BF16),
        grid=(rows // TM_FFN,),
        in_specs=[pl.BlockSpec((TM_FFN, f2), lambda i: (i, 0)), pl.BlockSpec(perm.shape, lambda i: (0, 0))],
        out_specs=pl.BlockSpec((TM_FFN, f2), lambda i: (i, 0)),
        compiler_params=_params("parallel"),
        name="w1_deinterleave",
    )(w1.reshape(rows, f2), perm)
    return out.reshape(w1.shape)


def _rows_from_tiles(ref):
    return jnp.concatenate([ref[:, s, :] for s in range(SUBLANES)], axis=1)


def _moe_ffn_kernel(te_ref, nu_ref, hs_ref, w1_ref, b1_ref, w2_ref, b2_ref, ys_ref):
    t = pl.program_id(0)

    @pl.when(t < nu_ref[0])
    def _():
        x = jnp.concatenate([hs_ref[pl.ds(s, TM_FFN, stride=SUBLANES), :] for s in range(SUBLANES)],
                            axis=1).astype(BF16)
        u = _dot(x, w1_ref[0]) + b1_ref[0]
        f = u.shape[-1] // 2
        glu = jnp.minimum(u[:, :f], SWIGLU_LIMIT)
        lin = jnp.clip(u[:, f:], -SWIGLU_LIMIT, SWIGLU_LIMIT)
        act = glu * _sigmoid(SWIGLU_ALPHA * glu) * (lin + 1.0)
        y = _dot(act.astype(BF16), w2_ref[0]) + b2_ref[0]
        for s in range(SUBLANES):
            ys_ref[pl.ds(s, TM_FFN, stride=SUBLANES), :] = y[:, s * LANES:(s + 1) * LANES]

    @pl.when(t >= nu_ref[0])
    def _():
        ys_ref[...] = jnp.zeros_like(ys_ref)


def _moe_ffn(tile_expert, n_used, hs3, w1, b1, w2, b2):
    n_rows = hs3.shape[0]
    n_tiles = n_rows // TM_FFN
    d = w1.shape[1]
    f2 = w1.shape[2]
    row_map = lambda t, te, nu: (jnp.minimum(t, nu[0] - 1), 0)
    exp_map = lambda t, te, nu: (te[t], 0, 0)
    grid_spec = pltpu.PrefetchScalarGridSpec(
        num_scalar_prefetch=2,
        grid=(n_tiles,),
        in_specs=[pl.BlockSpec((TM_FFN * SUBLANES, LANES), row_map),
                  pl.BlockSpec((1, d, f2), exp_map), pl.BlockSpec((1, 1, f2), exp_map),
                  pl.BlockSpec((1, f2 // 2, d), exp_map), pl.BlockSpec((1, 1, d), exp_map)],
        out_specs=pl.BlockSpec((TM_FFN * SUBLANES, LANES), lambda t, te, nu: (t, 0)),
    )
    ys2 = pl.pallas_call(
        _moe_ffn_kernel,
        out_shape=jax.ShapeDtypeStruct((n_rows * SUBLANES, LANES), F32),
        grid_spec=grid_spec,
        compiler_params=_params("arbitrary"),
        name="moe_ffn",
    )(tile_expert, n_used, hs3.reshape(n_rows * SUBLANES, LANES), w1, b1, w2, b2)
    return ys2.reshape(hs3.shape)


def _combine_kernel(pos_ref, p_ref, ys_hbm, x_ref, mod_ref, x_o, ybuf, abuf, sem):
    def issue(jj, carry):
        for kk in range(TOP_K):
            pltpu.make_async_copy(ys_hbm.at[pos_ref[jj * TOP_K + kk]], ybuf.at[kk * TM + jj], sem).start()
        return carry

    lax.fori_loop(0, TM, issue, 0)
    n = TM * TOP_K
    pltpu.make_async_copy(ys_hbm.at[pl.ds(0, n)], ybuf, sem).wait()

    def mix(jj, carry):
        acc = p_ref[jj * TOP_K] * ybuf[jj]
        for kk in range(1, TOP_K):
            acc = acc + p_ref[jj * TOP_K + kk] * ybuf[kk * TM + jj]
        abuf[jj] = acc
        return carry

    lax.fori_loop(0, TM, mix, 0, unroll=8)
    x_o[0] = x_ref[0] + mod_ref[0, 0][5:6] * _rows_from_tiles(abuf)


def _combine(pos_flat, p_flat, ys3, x, mod):
    bsz, tu, d = x.shape
    nt = tu // TM
    smem_spec = pl.BlockSpec((TM * TOP_K,), lambda b, i: (b * nt + i,), memory_space=pltpu.SMEM)
    return pl.pallas_call(
        _combine_kernel,
        out_shape=jax.ShapeDtypeStruct(x.shape, F32),
        grid=(bsz, nt),
        in_specs=[smem_spec, smem_spec, pl.BlockSpec(memory_space=pl.ANY), _row_spec(d), _mod_spec(d)],
        out_specs=_row_spec(d),
        scratch_shapes=[pltpu.VMEM((TM * TOP_K, SUBLANES, LANES), F32), pltpu.VMEM((TM, SUBLANES, LANES), F32),
                        pltpu.SemaphoreType.DMA(())],
        compiler_params=_params("arbitrary", "arbitrary"),
        name="moe_combine",
    )(pos_flat, p_flat, ys3, x, mod)


def _moe(x, hf, idx_pad, p_pad, mod, w1, b1, w2, b2):
    bsz, tu, d = x.shape
    n_tok = bsz * tu
    n_rows = n_tok * TOP_K + N_EXPERTS * TM_FFN
    n_tiles = n_rows // TM_FFN
    idx = idx_pad.reshape(n_tok, LANES)[:, :TOP_K]
    onehot = jnp.sum((idx[:, :, None] == jnp.arange(N_EXPERTS, dtype=jnp.int32)).astype(jnp.int32), axis=1)
    csum = jnp.cumsum(onehot, axis=0)
    counts = csum[-1]
    padded = ((counts + TM_FFN - 1) // TM_FFN) * TM_FFN
    ends = jnp.cumsum(padded)
    pos = (ends - padded)[idx] + jnp.take_along_axis(csum - onehot, idx, axis=1)
    n_used = (ends[-1] // TM_FFN).astype(jnp.int32)
    tile_start = jnp.arange(n_tiles, dtype=jnp.int32) * TM_FFN
    tile_expert = jnp.sum((tile_start[:, None] >= ends[None, :]).astype(jnp.int32), axis=1)
    last = jnp.take(tile_expert, jnp.maximum(n_used - 1, 0))
    tile_expert = jnp.where(jnp.arange(n_tiles) < n_used, tile_expert, last).astype(jnp.int32)
    pos_flat = pos.reshape(-1).astype(jnp.int32)

    hs3 = _dispatch(pos_flat, hf.reshape(n_tok, d), n_rows)
    ys3 = _moe_ffn(tile_expert, n_used.reshape(1), hs3, w1, b1, w2, b2)
    p_flat = p_pad.reshape(n_tok, LANES)[:, :TOP_K].reshape(-1)
    return _combine(pos_flat, p_flat, ys3, x, mod)


def _rope_tables(n_ctx, n_lat):
    def angles(rot_dim):
        rows = n_lat // GRID_W
        row = jnp.repeat(jnp.arange(rows, dtype=F32), GRID_W)
        col = jnp.tile(jnp.arange(GRID_W, dtype=F32), rows)
        half = rot_dim // 2
        inv_freq = ROPE_THETA ** (-jnp.arange(0, half, 2, dtype=F32) / half)
        return row[:, None] * inv_freq[None, :], col[:, None] * inv_freq[None, :]

    def with_ctx(t, fill):
        return jnp.concatenate([jnp.full((n_ctx, LANES), fill, F32), t], axis=0)

    ar, ac = angles(MLA_ROPE)
    one = jnp.ones((n_lat, 32), F32)
    zero = jnp.zeros((n_lat, 32), F32)
    cm = jnp.concatenate([jnp.cos(ar), jnp.cos(ac), one, jnp.cos(ar), jnp.cos(ac), one], axis=1)
    sm = jnp.concatenate([-jnp.sin(ar), -jnp.sin(ac), zero, jnp.sin(ar), jnp.sin(ac), zero], axis=1)
    ar, ac = angles(GQA_HEAD_DIM)
    cg = jnp.concatenate([jnp.cos(ar), jnp.cos(ac), jnp.cos(ar), jnp.cos(ac)], axis=1)
    sg = jnp.concatenate([-jnp.sin(ar), -jnp.sin(ac), jnp.sin(ar), jnp.sin(ac)], axis=1)
    return with_ctx(cm, 1.0), with_ctx(sm, 0.0), with_ctx(cg, 1.0), with_ctx(sg, 0.0)


_PERM_GQA = np.concatenate([np.arange(0, 32), np.arange(64, 96), np.arange(32, 64), np.arange(96, 128)])
_SRC_MLA = np.zeros(LANES, np.int32)
_MASK_MLA = np.zeros(LANES, np.float32)
_SRC_MLA[0:16], _SRC_MLA[16:32], _SRC_MLA[64:80], _SRC_MLA[80:96] = (
    np.arange(0, 16), np.arange(32, 48), np.arange(16, 32), np.arange(48, 64))
_MASK_MLA[0:32] = 1.0
_MASK_MLA[64:96] = 1.0


def _pad_rope64(w):
    return jnp.take(w, _SRC_MLA, axis=-1) * _MASK_MLA


def _attn_weights(e, w_in, q_norm, w_uq, kv_norm, w_ukv, qn_g, qr_g, kn_g, kr_g, gq_g, gk_g, w_out):
    s1 = MLA_Q_RANK
    s2 = s1 + MLA_KV_RANK
    s3 = s2 + MLA_ROPE
    s4 = s3 + GQA_HEADS * GQA_HEAD_DIM
    s5 = s4 + GQA_KV_HEADS * GQA_HEAD_DIM
    wi = w_in[e]
    d = wi.shape[0]
    gq = wi[:, s3:s4].reshape(d, GQA_HEADS, GQA_HEAD_DIM)[:, :, _PERM_GQA].reshape(d, -1)
    gk = wi[:, s4:s5].reshape(d, GQA_KV_HEADS, GQA_HEAD_DIM)[:, :, _PERM_GQA].reshape(d, -1)
    w_in_p = jnp.concatenate([wi[:, :s2], _pad_rope64(wi[:, s2:s3]), gq, gk, wi[:, s5:]], axis=1)
    uq = w_uq[e].reshape(MLA_Q_RANK, MLA_HEADS, MLA_NOPE + MLA_ROPE)
    uq_p = jnp.concatenate([uq[:, :, :MLA_NOPE], _pad_rope64(uq[:, :, MLA_NOPE:])], axis=-1)
    row = lambda g: g.reshape(1, -1).astype(F32)
    return dict(
        w_in=w_in_p.astype(BF16), q_norm=row(q_norm[e]), w_uq=uq_p.reshape(MLA_Q_RANK, -1).astype(BF16),
        kv_norm=row(kv_norm[e]), w_ukv=w_ukv[e].astype(BF16),
        qn_g=row(qn_g[e]), qr_g=row(_pad_rope64(qr_g[e])), kn_g=row(kn_g[e]), kr_g=row(_pad_rope64(kr_g[e])),
        gq_g=row(gq_g[e][_PERM_GQA]), gk_g=row(gk_g[e][_PERM_GQA]), w_out=w_out[e].astype(BF16))


def _pad_cols(w, n):
    return jnp.pad(w, ((0, 0), (0, n - w.shape[1])))


def _pad_rows(w, n):
    return jnp.pad(w, ((0, n - w.shape[0]), (0, 0)))


def _lora_pair(w1, w2):
    rank = w1.shape[-1]
    down = jnp.concatenate([w1[0], w1[1]], axis=1)
    up = jnp.stack([jnp.pad(w2[0], ((0, rank), (0, 0))), jnp.pad(w2[1], ((rank, 0), (0, 0)))])
    return down.astype(BF16), up.astype(BF16)


def _rwkv_weights(j, mix, w_r, w_k, w_v, w_o, w0, w1, w2, a0, a1, a2, v0, v1, v2, g1, g2, k_k, k_a, r_k, ln_w, ln_b):
    d = w_r.shape[1]
    row = lambda g: g.reshape(1, -1).astype(F32)
    heads = np.arange(d) // RWKV_HEAD
    hs1 = (heads[:, None] == np.arange(LANES)[None, :]).astype(np.float32)
    w1c, w2p = _lora_pair(w1[j], w2[j])
    a1c, a2p = _lora_pair(a1[j], a2[j])
    rp = dict(mix=mix[j], w_r=w_r[j].astype(BF16), w_k=w_k[j].astype(BF16), w_v=w_v[j].astype(BF16),
              w_o=w_o[j].astype(BF16), w0=w0[j], w1=w1c, w2=w2p, a0=a0[j], a1=a1c, a2=a2p,
              g1=g1[j].astype(BF16), g2=g2[j].astype(BF16), k_k=row(k_k[j]), k_a=row(k_a[j]), r_k=row(r_k[j]),
              ln_w=row(ln_w[j]), ln_b=row(ln_b[j]),
              hs1=jnp.asarray(hs1, BF16), hs2=jnp.asarray(hs1.T, BF16))
    if j > 0:
        rp.update(v0=row(v0[j - 1]), v1=_pad_cols(v1[j - 1], LANES).astype(BF16),
                  v2=_pad_rows(v2[j - 1], LANES).astype(BF16))
    return rp


def kernel(x, c, ctx, c_ctx, ada_w, ada_b, norm_mix, norm_ffn, attn_w_in, mla_q_norm, mla_w_uq, mla_kv_norm, mla_w_ukv, mla_qn_g, mla_qr_g, mla_kn_g, mla_kr_g, gqa_q_g, gqa_k_g, attn_w_out, rwkv_mix, rwkv_w_r, rwkv_w_k, rwkv_w_v, rwkv_w_o, rwkv_w0, rwkv_w1, rwkv_w2, rwkv_a0, rwkv_a1, rwkv_a2, rwkv_v0, rwkv_v1, rwkv_v2, rwkv_g1, rwkv_g2, rwkv_k_k, rwkv_k_a, rwkv_r_k, rwkv_ln_w, rwkv_ln_b, moe_router_w, moe_router_b, moe_w1, moe_b1, moe_w2, moe_b2):
    bsz, n_lat, d = x.shape
    n_ctx = ctx.shape[1]
    depth = ada_w.shape[0]
    tu = n_ctx + n_lat
    assert n_ctx % TM == 0 and n_lat % TM == 0 and n_ctx % WKV_CHUNK == 0 and d == SUBLANES * LANES
    assert (bsz * tu * TOP_K) % TM_FFN == 0

    rows = -(-(bsz + 1) // SUBLANES) * SUBLANES
    cond = jnp.zeros((rows, d), F32).at[:bsz].set(c).at[bsz].set(c_ctx)
    mods = _adaln(cond, ada_w, ada_b)
    mods_lat = mods[:, :bsz].reshape(depth, bsz, 1, N_MOD, d)
    mods_ctx = jnp.broadcast_to(mods[:, bsz].reshape(depth, 1, 1, N_MOD, d), mods_lat.shape)
    mods = jnp.concatenate([mods_ctx, mods_lat], axis=2)

    tabs = _rope_tables(n_ctx, n_lat)
    xs = jnp.concatenate([ctx, x], axis=1)

    n_e, _, f2 = moe_w1.shape[1:]
    rw_all = jnp.pad(moe_router_w, ((0, 0), (0, 0), (0, LANES - n_e)))
    rb_all = jnp.pad(moe_router_b, ((0, 0), (0, LANES - n_e)), constant_values=-1e30).reshape(depth, 1, LANES)

    w1_all = _deinterleave_w1(moe_w1)

    v_first = None
    for i in range(depth):
        mod = mods[i]
        nm = norm_mix[i].reshape(1, d)
        nf = norm_ffn[i].reshape(1, d)
        rw, rb = rw_all[i], rb_all[i]
        if i % 2 == 0:
            aw = _attn_weights(i // 2, attn_w_in, mla_q_norm, mla_w_uq, mla_kv_norm, mla_w_ukv, mla_qn_g,
                               mla_qr_g, mla_kn_g, mla_kr_g, gqa_q_g, gqa_k_g, attn_w_out)
            qm, km, vm, qg, kg, vg = _attn_proj(xs, mod, nm, aw, tabs)
            om = _attention(qm, km, vm, n_ctx)
            og = _attention(qg, kg, vg, n_ctx)
            xs, hf, idx_pad, p_pad = _attn_out(om, og, xs, mod, aw["w_out"], nf, rw, rb)
        else:
            j = i // 2
            rp = _rwkv_weights(j, rwkv_mix, rwkv_w_r, rwkv_w_k, rwkv_w_v, rwkv_w_o, rwkv_w0, rwkv_w1, rwkv_w2,
                               rwkv_a0, rwkv_a1, rwkv_a2, rwkv_v0, rwkv_v1, rwkv_v2, rwkv_g1, rwkv_g2,
                               rwkv_k_k, rwkv_k_a, rwkv_r_k, rwkv_ln_w, rwkv_ln_b)
            r, v, kk, w, k, b, g, bv = _rwkv_feat(xs, mod, nm, rp, v_first if j > 0 else None)
            if j == 0:
                v_first = v
            y = _wkv_scan(r, v, kk, w, k, b, n_ctx)
            xs, hf, idx_pad, p_pad = _rwkv_out(y, g, bv, xs, mod, rp, nf, rw, rb)
        w1 = w1_all[i]
        b1 = jnp.concatenate([moe_b1[i][:, 0::2], moe_b1[i][:, 1::2]], axis=-1).reshape(n_e, 1, f2)
        w2 = moe_w2[i].astype(BF16)
        b2 = moe_b2[i].reshape(n_e, 1, d)
        xs = _moe(xs, hf, idx_pad, p_pad, mod, w1, b1, w2, b2)
    return xs[:, n_ctx:]
```

```python
import functools

import jax
import jax.numpy as jnp
import numpy as np
from jax import lax
from jax.experimental import pallas as pl
from jax.experimental.pallas import tpu as pltpu

F32 = jnp.float32
BF16 = jnp.bfloat16
HIGHEST = lax.Precision.HIGHEST

SUBLANES = 8
LANES = 128
VMEM_LIMIT = 56 * 1024 * 1024

GRID_W = 64
ROPE_THETA = 10000.0
NORM_EPS = 1e-6
N_MOD = 6

MLA_HEADS = 4
MLA_Q_RANK = 384
MLA_KV_RANK = 256
MLA_NOPE = 128
MLA_ROPE = 64
MLA_V = 128
MLA_SCALE = (MLA_NOPE + MLA_ROPE) ** -0.5
GQA_HEADS = 4
GQA_KV_HEADS = 2
GQA_HEAD_DIM = 128
GQA_SCALE = GQA_HEAD_DIM ** -0.5
LOG2E = float(np.log2(np.e))

RWKV_HEAD = 64
RWKV_GN_EPS = 64e-5
N_MIX = 6
DECAY_SCALE = float(np.exp(-0.5))

N_EXPERTS = 32
TOP_K = 4
SWIGLU_ALPHA = 1.702
SWIGLU_LIMIT = 7.0

TM = 256
TM_FFN = 512
WKV_CHUNK = 64
WKV_ROWS = 4


def _params(*sem):
    return pltpu.CompilerParams(dimension_semantics=sem, vmem_limit_bytes=VMEM_LIMIT)


def _dot(a, b):
    return jnp.dot(a, b, preferred_element_type=F32)


def _bdot(a, b_ref):
    return jnp.dot(a.astype(BF16), b_ref[...], preferred_element_type=F32)


def _rms(x, g, n):
    ms = jnp.sum(x * x, axis=-1, keepdims=True) * (1.0 / n)
    return x * lax.rsqrt(ms + NORM_EPS) * g


def _sigmoid(x):
    return 1.0 / (1.0 + jnp.exp(-x))


def _rope(x, cos, sin):
    return x * cos + pltpu.roll(x, 64, axis=1) * sin


def _adaln_kernel(c_ref, w_ref, b_ref, o_ref):
    c = c_ref[...]
    s = c * _sigmoid(c)
    o_ref[0] = jnp.dot(s, w_ref[0], precision=HIGHEST, preferred_element_type=F32) + b_ref[0]


def _adaln(cond, ada_w, ada_b):
    depth, d, nd = ada_w.shape
    rows = cond.shape[0]
    return pl.pallas_call(
        _adaln_kernel,
        out_shape=jax.ShapeDtypeStruct((depth, rows, nd), F32),
        grid=(depth, nd // d),
        in_specs=[pl.BlockSpec((rows, d), lambda l, n: (0, 0)),
                  pl.BlockSpec((1, d, d), lambda l, n: (l, 0, n)),
                  pl.BlockSpec((1, 1, d), lambda l, n: (l, 0, n))],
        out_specs=pl.BlockSpec((1, rows, d), lambda l, n: (l, 0, n)),
        compiler_params=_params("arbitrary", "arbitrary"),
        name="adaln",
    )(cond, ada_w, ada_b.reshape(depth, 1, nd))


def _ffn_tail(x2, mod, nf_ref, rw_ref, rb_ref, hf_o, idx_o, p_o):
    d = x2.shape[-1]
    hf = _rms(x2, nf_ref[...], d) * (1.0 + mod[4:5]) + mod[3:4]
    hf_o[0] = hf
    logits = jnp.dot(hf, rw_ref[...], precision=HIGHEST, preferred_element_type=F32) + rb_ref[...]
    lane = lax.broadcasted_iota(jnp.int32, logits.shape, 1).astype(F32)
    vals, idxs = [], []
    l = logits
    for _ in range(TOP_K):
        m = jnp.max(l, axis=-1, keepdims=True)
        ik = jnp.min(jnp.where(l == m, lane, float(LANES)), axis=-1, keepdims=True)
        vals.append(m)
        idxs.append(ik)
        l = jnp.where(lane == ik, -jnp.inf, l)
    es = [jnp.exp(v - vals[0]) for v in vals]
    inv = 1.0 / (es[0] + es[1] + es[2] + es[3])
    idx_acc = jnp.zeros_like(logits)
    p_acc = jnp.zeros_like(logits)
    for k in range(TOP_K):
        idx_acc = jnp.where(lane == float(k), idxs[k], idx_acc)
        p_acc = jnp.where(lane == float(k), es[k] * inv, p_acc)
    idx_o[0] = idx_acc.astype(jnp.int32)
    p_o[0] = p_acc


def _attn_proj_kernel(x_ref, mod_ref, nm_ref, win_ref, qn_ref, wuq_ref, kvn_ref, wukv_ref,
                      qng_ref, qrg_ref, kng_ref, krg_ref, gqg_ref, gkg_ref,
                      cm_ref, sm_ref, cg_ref, sg_ref,
                      qm_o, km_o, vm_o, qg_o, kg_o, vg_o):
    x = x_ref[0]
    mod = mod_ref[0, 0]
    d = x.shape[-1]
    h = _rms(x, nm_ref[...], d) * (1.0 + mod[1:2]) + mod[0:1]
    z = _bdot(h, win_ref)
    o1 = MLA_Q_RANK
    o2 = o1 + MLA_KV_RANK
    o3 = o2 + LANES
    o4 = o3 + GQA_HEADS * GQA_HEAD_DIM
    o5 = o4 + GQA_KV_HEADS * GQA_HEAD_DIM
    q = _bdot(_rms(z[:, :o1], qn_ref[...], MLA_Q_RANK), wuq_ref)
    kv = _bdot(_rms(z[:, o1:o2], kvn_ref[...], MLA_KV_RANK), wukv_ref)
    cm, sm, cg, sg = cm_ref[...], sm_ref[...], cg_ref[...], sg_ref[...]
    kr = _rope(_rms(z[:, o2:o3], krg_ref[...], MLA_ROPE), cm, sm)
    for hh in range(MLA_HEADS):
        b0 = hh * 2 * LANES
        qn = _rms(q[:, b0:b0 + LANES], qng_ref[...], MLA_NOPE)
        qr = _rope(_rms(q[:, b0 + LANES:b0 + 2 * LANES], qrg_ref[...], MLA_ROPE), cm, sm)
        qm_o[0, hh] = (jnp.concatenate([qn, qr], axis=1) * (MLA_SCALE * LOG2E)).astype(BF16)
        kn = _rms(kv[:, b0:b0 + LANES], kng_ref[...], MLA_NOPE)
        km_o[0, hh] = jnp.concatenate([kn, kr], axis=1).astype(BF16)
        vm_o[0, hh] = kv[:, b0 + LANES:b0 + 2 * LANES].astype(BF16)
    for hh in range(GQA_HEADS):
        g = z[:, o3 + hh * LANES:o3 + (hh + 1) * LANES]
        qg_o[0, hh] = (_rope(_rms(g, gqg_ref[...], GQA_HEAD_DIM), cg, sg) * (GQA_SCALE * LOG2E)).astype(BF16)
    for hh in range(GQA_KV_HEADS):
        g = z[:, o4 + hh * LANES:o4 + (hh + 1) * LANES]
        kg_o[0, hh] = _rope(_rms(g, gkg_ref[...], GQA_HEAD_DIM), cg, sg).astype(BF16)
        vg_o[0, hh] = z[:, o5 + hh * LANES:o5 + (hh + 1) * LANES].astype(BF16)


def _full(shape):
    nd = len(shape)
    return pl.BlockSpec(shape, lambda b, i: (0,) * nd)


def _mod_spec(d):
    return pl.BlockSpec((1, 1, N_MOD, d), lambda b, i: (b, jnp.minimum(i, 1), 0, 0))


def _row_spec(d):
    return pl.BlockSpec((1, TM, d), lambda b, i: (b, i, 0))


def _attn_proj(x, mod, nm, aw, tabs):
    bsz, tu, d = x.shape
    nt = tu // TM
    head_spec = lambda nh, w: pl.BlockSpec((1, nh, TM, w), lambda b, i: (b, 0, i, 0))
    tab_spec = pl.BlockSpec((TM, LANES), lambda b, i: (i, 0))
    weights = [nm, aw["w_in"], aw["q_norm"], aw["w_uq"], aw["kv_norm"], aw["w_ukv"],
               aw["qn_g"], aw["qr_g"], aw["kn_g"], aw["kr_g"], aw["gq_g"], aw["gk_g"]]
    out_shape = [
        jax.ShapeDtypeStruct((bsz, MLA_HEADS, tu, 2 * LANES), BF16),
        jax.ShapeDtypeStruct((bsz, MLA_HEADS, tu, 2 * LANES), BF16),
        jax.ShapeDtypeStruct((bsz, MLA_HEADS, tu, LANES), BF16),
        jax.ShapeDtypeStruct((bsz, GQA_HEADS, tu, LANES), BF16),
        jax.ShapeDtypeStruct((bsz, GQA_KV_HEADS, tu, LANES), BF16),
        jax.ShapeDtypeStruct((bsz, GQA_KV_HEADS, tu, LANES), BF16),
    ]
    return pl.pallas_call(
        _attn_proj_kernel,
        out_shape=out_shape,
        grid=(bsz, nt),
        in_specs=[_row_spec(d), _mod_spec(d)] + [_full(w.shape) for w in weights] + [tab_spec] * 4,
        out_specs=[head_spec(MLA_HEADS, 2 * LANES), head_spec(MLA_HEADS, 2 * LANES),
                   head_spec(MLA_HEADS, LANES), head_spec(GQA_HEADS, LANES),
                   head_spec(GQA_KV_HEADS, LANES), head_spec(GQA_KV_HEADS, LANES)],
        compiler_params=_params("parallel", "parallel"),
        name="attn_proj",
    )(x, mod, *weights, *tabs)


def _attn_kernel(n_ctx, q_ref, k_ref, v_ref, o_ref):
    q = q_ref[0, 0]

    def attend(k, v):
        s = lax.dot_general(q, k, (((1,), (1,)), ((), ())), preferred_element_type=F32)
        m = jnp.max(s, axis=-1, keepdims=True)
        p = jnp.exp2(s - m)
        l = jnp.sum(p, axis=-1, keepdims=True)
        o = _dot(p.astype(BF16), v)
        o_ref[0] = (o * (1.0 / l)).astype(o_ref.dtype)

    is_ctx = pl.program_id(2) == 0

    @pl.when(is_ctx)
    def _():
        attend(k_ref[0, 0, :n_ctx], v_ref[0, 0, :n_ctx])

    @pl.when(jnp.logical_not(is_ctx))
    def _():
        attend(k_ref[0, 0], v_ref[0, 0])


def _attention(q, k, v, n_ctx):
    bsz, hq, tu, dk = q.shape
    hk = k.shape[1]
    grp = hq // hk
    dv = v.shape[-1]
    return pl.pallas_call(
        functools.partial(_attn_kernel, n_ctx),
        out_shape=jax.ShapeDtypeStruct((bsz, tu, hq * dv), BF16),
        grid=(bsz, hq, tu // TM),
        in_specs=[pl.BlockSpec((1, 1, TM, dk), lambda b, h, i: (b, h, i, 0)),
                  pl.BlockSpec((1, 1, tu, dk), lambda b, h, i: (b, h // grp, 0, 0)),
                  pl.BlockSpec((1, 1, tu, dv), lambda b, h, i: (b, h // grp, 0, 0))],
        out_specs=pl.BlockSpec((1, TM, dv), lambda b, h, i: (b, i, h)),
        compiler_params=_params("parallel", "parallel", "arbitrary"),
        name="attention",
    )(q, k, v)


def _attn_out_kernel(om_ref, og_ref, x_ref, mod_ref, wo_ref, nf_ref, rw_ref, rb_ref,
                     x_o, hf_o, idx_o, p_o):
    mod = mod_ref[0, 0]
    nm = om_ref.shape[-1]
    o = _dot(om_ref[0], wo_ref[:nm]) + _dot(og_ref[0], wo_ref[nm:])
    x2 = x_ref[0] + mod[2:3] * o
    x_o[0] = x2
    _ffn_tail(x2, mod, nf_ref, rw_ref, rb_ref, hf_o, idx_o, p_o)


def _tail_out_shapes(bsz, tu, d):
    return [jax.ShapeDtypeStruct((bsz, tu, d), F32), jax.ShapeDtypeStruct((bsz, tu, d), F32),
            jax.ShapeDtypeStruct((bsz, tu, LANES), jnp.int32), jax.ShapeDtypeStruct((bsz, tu, LANES), F32)]


def _tail_out_specs(d):
    return [_row_spec(d), _row_spec(d), _row_spec(LANES), _row_spec(LANES)]


def _attn_out(om, og, x, mod, wo, nf, rw, rb):
    bsz, tu, d = x.shape
    return pl.pallas_call(
        _attn_out_kernel,
        out_shape=_tail_out_shapes(bsz, tu, d),
        grid=(bsz, tu // TM),
        in_specs=[_row_spec(om.shape[-1]), _row_spec(og.shape[-1]), _row_spec(d), _mod_spec(d),
                  _full(wo.shape), _full(nf.shape), _full(rw.shape), _full(rb.shape)],
        out_specs=_tail_out_specs(d),
        compiler_params=_params("parallel", "parallel"),
        name="attn_out",
    )(om, og, x, mod, wo, nf, rw, rb)


def _head_sum(x, hs1_ref, hs2_ref):
    hi = x.astype(BF16)
    lo = (x - hi.astype(F32)).astype(BF16)
    s = _dot(hi, hs1_ref[...]) + _dot(lo, hs1_ref[...])
    shi = s.astype(BF16)
    slo = (s - shi.astype(F32)).astype(BF16)
    return _dot(shi, hs2_ref[...]) + _dot(slo, hs2_ref[...])


def _rwkv_feat_kernel(has_vres, nt, x_ref, xp_ref, xn_ref, mod_ref, nm_ref, mix_ref,
                      wr_ref, wk_ref, wv_ref, w0_ref, w1_ref, w2_ref, a0_ref, a1_ref, a2_ref,
                      g1_ref, g2_ref, kk_ref, ka_ref, rk_ref, hs1_ref, hs2_ref, *rest):
    if has_vres:
        v0_ref, v1_ref, v2_ref, vf_ref = rest[:4]
        rest = rest[4:]
    r_o, v_o, kk_o, w_o, k_o, b_o, g_o, bv_o = rest
    i = pl.program_id(1)
    mod = mod_ref[0, 0]
    d = x_ref.shape[-1]
    nm = nm_ref[...]

    def hmod(xx):
        return _rms(xx, nm, d) * (1.0 + mod[1:2]) + mod[0:1]

    h = hmod(x_ref[0])
    has_prev = (i >= 2).astype(F32)
    has_next = jnp.logical_and(i >= 1, i < nt - 1).astype(F32)
    prev_row = hmod(xp_ref[0])[SUBLANES - 1:SUBLANES] * has_prev
    next_row = hmod(xn_ref[0])[0:1] * has_next
    row = lax.broadcasted_iota(jnp.int32, h.shape, 0)
    hp = jnp.where(row == 0, prev_row, pltpu.roll(h, 1, axis=0))
    hn = jnp.where(row == TM - 1, next_row, pltpu.roll(h, TM - 1, axis=0))
    xx = 0.5 * (hp + hn) - h
    xr, xw, xk, xv, xa, xg = (h + xx * mix_ref[m:m + 1] for m in range(N_MIX))

    r = _bdot(xr, wr_ref)
    k = _bdot(xk, wk_ref)
    v = _bdot(xv, wv_ref)
    if has_vres:
        gate = _sigmoid(v0_ref[...] + _bdot(_bdot(xv, v1_ref), v2_ref))
        v = v + (vf_ref[0] - v) * gate
    kk = k * kk_ref[...]
    kk = kk * lax.rsqrt(jnp.maximum(_head_sum(kk * kk, hs1_ref, hs2_ref), 1e-24))
    tw = jnp.tanh(_bdot(xw, w1_ref)).astype(BF16)
    ta = _bdot(xa, a1_ref).astype(BF16)
    ksum = jnp.zeros_like(k)
    for dd in range(2):
        z = w0_ref[dd:dd + 1] + _dot(tw, w2_ref[dd])
        w_o[dd, 0] = -DECAY_SCALE * _sigmoid(z)
        a = _sigmoid(a0_ref[dd:dd + 1] + _dot(ta, a2_ref[dd]))
        kd = k * (1.0 + (a - 1.0) * ka_ref[...])
        k_o[dd, 0] = kd
        b_o[dd, 0] = kk * a
        ksum = ksum + kd
    g_o[0] = _bdot(_sigmoid(_bdot(xg, g1_ref)), g2_ref)
    bv_o[0] = _head_sum(r * ksum * rk_ref[...], hs1_ref, hs2_ref) * v
    r_o[0] = r
    v_o[0] = v
    kk_o[0] = kk


def _rwkv_feat(x, mod, nm, rp, v_first):
    bsz, tu, d = x.shape
    nt = tu // TM
    has_vres = v_first is not None
    nblk8 = tu // SUBLANES
    per8 = TM // SUBLANES
    prev_spec = pl.BlockSpec((1, SUBLANES, d), lambda b, i: (b, jnp.maximum(i * per8 - 1, 0), 0))
    next_spec = pl.BlockSpec((1, SUBLANES, d), lambda b, i: (b, jnp.minimum((i + 1) * per8, nblk8 - 1), 0))
    weights = [nm, rp["mix"], rp["w_r"], rp["w_k"], rp["w_v"], rp["w0"], rp["w1"], rp["w2"],
               rp["a0"], rp["a1"], rp["a2"], rp["g1"], rp["g2"], rp["k_k"], rp["k_a"], rp["r_k"],
               rp["hs1"], rp["hs2"]]
    extra, extra_specs = [], []
    if has_vres:
        extra = [rp["v0"], rp["v1"], rp["v2"], v_first]
        extra_specs = [_full(rp["v0"].shape), _full(rp["v1"].shape), _full(rp["v2"].shape), _row_spec(d)]
    one = jax.ShapeDtypeStruct((bsz, tu, d), F32)
    two = jax.ShapeDtypeStruct((2, bsz, tu, d), F32)
    dir_spec = pl.BlockSpec((2, 1, TM, d), lambda b, i: (0, b, i, 0))
    return pl.pallas_call(
        functools.partial(_rwkv_feat_kernel, has_vres, nt),
        out_shape=[one, one, one, two, two, two, one, one],
        grid=(bsz, nt),
        in_specs=[_row_spec(d), prev_spec, next_spec, _mod_spec(d)] + [_full(w.shape) for w in weights] + extra_specs,
        out_specs=[_row_spec(d)] * 3 + [dir_spec] * 3 + [_row_spec(d)] * 2,
        compiler_params=_params("parallel", "parallel"),
        name="rwkv_feat",
    )(x, x, x, mod, *weights, *extra)


def _dot_nt(a, b):
    return lax.dot_general(a, b, (((1,), (1,)), ((), ())), preferred_element_type=F32)


def _wkv_kernel(r_ref, v_ref, kk_ref, lw_ref, k_ref, b_ref, cum_ref, msk_ref, y_ref, st_scr):
    @pl.when(pl.program_id(2) == 0)
    def _():
        st_scr[...] = jnp.zeros_like(st_scr)

    n_rows = r_ref.shape[0]
    n_pairs = r_ref.shape[-1] // LANES
    cum = cum_ref[0]
    at, rt, kt, bt, vs, lam_tot = [], [], [], [], [], []
    for bi in range(n_rows):
        lw = lw_ref[0, bi]
        lw_hi = lw.astype(BF16)
        lw_lo = (lw - lw_hi.astype(F32)).astype(BF16)
        cs = _dot(cum, lw_hi) + _dot(cum, lw_lo)
        lam_inv = jnp.exp(-cs)
        rt.append(r_ref[bi] * jnp.exp(cs))
        at.append(-kk_ref[bi] * jnp.exp(cs - lw))
        kt.append(k_ref[0, bi] * lam_inv)
        bt.append(b_ref[0, bi] * lam_inv)
        vs.append(v_ref[bi])
        lam_tot.append(jnp.exp(jnp.sum(lw, axis=0, keepdims=True)))

    strict = msk_ref[0, 0][:, :WKV_CHUNK] > 0.5
    m_ak = msk_ref[0, 1] > 0.5
    m_y = msk_ref[0, 2] > 0.5
    head_of_lane = lax.broadcasted_iota(jnp.int32, (1, LANES), 1) // RWKV_HEAD
    rowi = lax.broadcasted_iota(jnp.int32, (LANES, LANES), 0)
    coli = lax.broadcasted_iota(jnp.int32, (LANES, LANES), 1)
    blockdiag = (rowi // RWKV_HEAD) == (coli // RWKV_HEAD)

    hpp = LANES // RWKV_HEAD
    n_sq = WKV_CHUNK.bit_length() - 1
    pairs = [(bi, p) for bi in range(n_rows) for p in range(n_pairs)]
    heads = [(q, hh) for q in pairs for hh in range(hpp)]
    sls = {q: slice(q[1] * LANES, (q[1] + 1) * LANES) for q in pairs}
    ar = {q: jnp.concatenate([at[q[0]][:, sls[q]], rt[q[0]][:, sls[q]]], axis=0) for q in pairs}
    bk = {q: jnp.concatenate([bt[q[0]][:, sls[q]], kt[q[0]][:, sls[q]]], axis=0).astype(BF16) for q in pairs}
    vp = {q: vs[q[0]][:, sls[q]] for q in pairs}
    vv = {q: jnp.concatenate([vp[q], vp[q]], axis=0).astype(BF16) for q in pairs}
    st = {q: st_scr[q[0] * n_pairs + q[1]] for q in pairs}
    rhs = {q: jnp.concatenate([bk[q], st[q].astype(BF16)], axis=0) for q in pairs}
    arm = {h: jnp.where(head_of_lane == h[1], ar[h[0]], 0.0).astype(BF16) for h in heads}
    gs = {h: _dot_nt(arm[h], rhs[h[0]]) for h in heads}
    g = {h: gs[h][:, :2 * WKV_CHUNK] for h in heads}
    s0 = {h: gs[h][:, 2 * WKV_CHUNK:] for h in heads}
    u = {h: _dot(jnp.where(m_ak, g[h][:WKV_CHUNK], 0.0).astype(BF16), vv[h[0]]) + s0[h][:WKV_CHUNK]
         for h in heads}
    pk = {h: jnp.where(strict, g[h][:WKV_CHUNK, :WKV_CHUNK], 0.0) for h in heads}
    for it in range(n_sq):
        pkb = {h: pk[h].astype(BF16) for h in heads}
        u = {h: u[h] + _dot(pkb[h], u[h].astype(BF16)) for h in heads}
        if it < n_sq - 1:
            pk = {h: _dot(pkb[h], pkb[h]) for h in heads}
    ys = {h: _dot(jnp.where(m_y, g[h][WKV_CHUNK:], 0.0).astype(BF16),
                  jnp.concatenate([u[h], vp[h[0]]], axis=0).astype(BF16)) + s0[h][WKV_CHUNK:]
          for h in heads}
    for q in pairs:
        y_ref[0, q[0], :, sls[q]] = jnp.where(head_of_lane == 0, ys[(q, 0)], ys[(q, 1)])
        u_pair = jnp.where(head_of_lane == 0, u[(q, 0)], u[(q, 1)])
        uvp = jnp.concatenate([u_pair, vp[q]], axis=0).astype(BF16)
        upd = lax.dot_general(uvp, bk[q], (((0,), (0,)), ((), ())), preferred_element_type=F32)
        st_scr[q[0] * n_pairs + q[1]] = jnp.where(blockdiag, (st[q] + upd) * lam_tot[q[0]][:, sls[q]], 0.0)


def _scan_tables():
    t = np.arange(WKV_CHUNK)
    incl = [t[None, :] <= t[:, None], t[None, :] >= t[:, None]]
    strict = [t[None, :] < t[:, None], t[None, :] > t[:, None]]
    msk = np.zeros((2, 3, WKV_CHUNK, 2 * WKV_CHUNK), np.float32)
    for dd in range(2):
        msk[dd, 0, :, :WKV_CHUNK] = strict[dd]
        msk[dd, 1, :, WKV_CHUNK:] = strict[dd]
        msk[dd, 2, :, :WKV_CHUNK] = incl[dd]
        msk[dd, 2, :, WKV_CHUNK:] = incl[dd]
    return jnp.asarray(np.stack(incl).astype(np.float32), BF16), jnp.asarray(msk, F32)


def _wkv_scan(r, v, kk, lw, k, b, n_ctx):
    bsz, tu, d = r.shape
    nb = tu // WKV_CHUNK
    cb = n_ctx // WKV_CHUNK
    cum, msk = _scan_tables()

    def blk(dd, j):
        back = jnp.where(j < cb, cb - 1 - j, nb - 1 + cb - j)
        return jnp.where(dd == 0, j, back)

    rows = WKV_ROWS if bsz % WKV_ROWS == 0 else 1
    shared = pl.BlockSpec((rows, WKV_CHUNK, d), lambda dd, bb, j: (bb, blk(dd, j), 0))
    per_dir = pl.BlockSpec((1, rows, WKV_CHUNK, d), lambda dd, bb, j: (dd, bb, blk(dd, j), 0))
    return pl.pallas_call(
        _wkv_kernel,
        out_shape=jax.ShapeDtypeStruct((2, bsz, tu, d), F32),
        grid=(2, bsz // rows, nb),
        in_specs=[shared, shared, shared, per_dir, per_dir, per_dir,
                  pl.BlockSpec((1, WKV_CHUNK, WKV_CHUNK), lambda dd, bb, j: (dd, 0, 0)),
                  pl.BlockSpec((1, 3, WKV_CHUNK, 2 * WKV_CHUNK), lambda dd, bb, j: (dd, 0, 0, 0))],
        out_specs=per_dir,
        scratch_shapes=[pltpu.VMEM((rows * (d // LANES), LANES, LANES), F32)],
        compiler_params=_params("arbitrary", "arbitrary", "arbitrary"),
        name="wkv_scan",
    )(r, v, kk, lw, k, b, cum, msk)


def _rwkv_out_kernel(y_ref, g_ref, bv_ref, x_ref, mod_ref, lnw_ref, lnb_ref, wo_ref, hs1_ref, hs2_ref,
                     nf_ref, rw_ref, rb_ref, x_o, hf_o, idx_o, p_o):
    mod = mod_ref[0, 0]
    y = y_ref[0, 0] + y_ref[1, 0]
    inv_n = 1.0 / RWKV_HEAD
    mu = _head_sum(y, hs1_ref, hs2_ref) * inv_n
    dlt = y - mu
    var = _head_sum(dlt * dlt, hs1_ref, hs2_ref) * inv_n
    yn = dlt * lax.rsqrt(var + RWKV_GN_EPS) * lnw_ref[...] + lnb_ref[...]
    o = _bdot((yn + bv_ref[0]) * g_ref[0], wo_ref)
    x2 = x_ref[0] + mod[2:3] * o
    x_o[0] = x2
    _ffn_tail(x2, mod, nf_ref, rw_ref, rb_ref, hf_o, idx_o, p_o)


def _rwkv_out(y, g, bv, x, mod, rp, nf, rw, rb):
    bsz, tu, d = x.shape
    weights = [rp["ln_w"], rp["ln_b"], rp["w_o"], rp["hs1"], rp["hs2"], nf, rw, rb]
    return pl.pallas_call(
        _rwkv_out_kernel,
        out_shape=_tail_out_shapes(bsz, tu, d),
        grid=(bsz, tu // TM),
        in_specs=[pl.BlockSpec((2, 1, TM, d), lambda b, i: (0, b, i, 0))] + [_row_spec(d)] * 3 + [_mod_spec(d)]
        + [_full(w.shape) for w in weights],
        out_specs=_tail_out_specs(d),
        compiler_params=_params("parallel", "parallel"),
        name="rwkv_out",
    )(y, g, bv, x, mod, *weights)


def _dispatch_kernel(pos_ref, h_ref, init_hbm, out_hbm, hbuf, sem):
    del init_hbm
    h = h_ref[...]
    for s in range(SUBLANES):
        hbuf[:, s, :] = h[:, s * LANES:(s + 1) * LANES]

    def issue(jj, carry):
        for kk in range(TOP_K):
            pltpu.make_async_copy(hbuf.at[jj], out_hbm.at[pos_ref[jj * TOP_K + kk]], sem).start()
        return carry

    lax.fori_loop(0, TM, issue, 0)
    for _ in range(TOP_K):
        pltpu.make_async_copy(hbuf, out_hbm.at[pl.ds(0, TM)], sem).wait()


def _dispatch(pos_flat, h2, n_rows):
    n_tok, d = h2.shape
    init = jnp.zeros((n_rows, SUBLANES, LANES), h2.dtype)
    return pl.pallas_call(
        _dispatch_kernel,
        out_shape=jax.ShapeDtypeStruct(init.shape, init.dtype),
        grid=(n_tok // TM,),
        in_specs=[pl.BlockSpec((TM * TOP_K,), lambda i: (i,), memory_space=pltpu.SMEM),
                  pl.BlockSpec((TM, d), lambda i: (i, 0)), pl.BlockSpec(memory_space=pl.ANY)],
        out_specs=pl.BlockSpec(memory_space=pl.ANY),
        scratch_shapes=[pltpu.VMEM((TM, SUBLANES, LANES), h2.dtype), pltpu.SemaphoreType.DMA(())],
        input_output_aliases={2: 0},
        compiler_params=_params("arbitrary"),
        name="moe_dispatch",
    )(pos_flat, h2, init)


def _deinterleave_kernel(w_ref, p_ref, o_ref):
    half = o_ref.shape[-1] // 2
    for j in range(half // LANES):
        blk = _dot(w_ref[:, 2 * LANES * j:2 * LANES * (j + 1)].astype(BF16), p_ref[...]).astype(BF16)
        o_ref[:, LANES * j:LANES * (j + 1)] = blk[:, :LANES]
        o_ref[:, half + LANES * j:half + LANES * (j + 1)] = blk[:, LANES:]


def _deinterleave_w1(w1):
    f2 = w1.shape[-1]
    rows = int(np.prod(w1.shape[:-1]))
    col = jnp.arange(2 * LANES, dtype=jnp.int32)
    src = jnp.where(col < LANES, 2 * col, 2 * (col - LANES) + 1)
    perm = (col[:, None] == src[None, :]).astype(BF16)
    out = pl.pallas_call(
        _deinterleave_kernel,
        out_shape=jax.ShapeDtypeStruct((rows, f2), BF16),
        grid=(rows // TM_FFN,),
        in_specs=[pl.BlockSpec((TM_FFN, f2), lambda i: (i, 0)), pl.BlockSpec(perm.shape, lambda i: (0, 0))],
        out_specs=pl.BlockSpec((TM_FFN, f2), lambda i: (i, 0)),
        compiler_params=_params("parallel"),
        name="w1_deinterleave",
    )(w1.reshape(rows, f2), perm)
    return out.reshape(w1.shape)


def _moe_ffn_kernel(te_ref, nu_ref, hs_ref, w1_ref, b1_ref, w2_ref, b2_ref, ys_ref):
    t = pl.program_id(0)

    @pl.when(t < nu_ref[0])
    def _():
        x = jnp.concatenate([hs_ref[pl.ds(s, TM_FFN, stride=SUBLANES), :] for s in range(SUBLANES)],
                            axis=1).astype(BF16)
        u = _dot(x, w1_ref[0]) + b1_ref[0]
        f = u.shape[-1] // 2
        glu = jnp.minimum(u[:, :f], SWIGLU_LIMIT)
        lin = jnp.clip(u[:, f:], -SWIGLU_LIMIT, SWIGLU_LIMIT)
        act = glu * _sigmoid(SWIGLU_ALPHA * glu) * (lin + 1.0)
        y = _dot(act.astype(BF16), w2_ref[0]) + b2_ref[0]
        for s in range(SUBLANES):
            ys_ref[pl.ds(s, TM_FFN, stride=SUBLANES), :] = y[:, s * LANES:(s + 1) * LANES]

    @pl.when(t >= nu_ref[0])
    def _():
        ys_ref[...] = jnp.zeros_like(ys_ref)


def _moe_ffn(tile_expert, n_used, hs3, w1, b1, w2, b2):
    n_rows = hs3.shape[0]
    n_tiles = n_rows // TM_FFN
    d = w1.shape[1]
    f2 = w1.shape[2]
    row_map = lambda t, te, nu: (jnp.minimum(t, nu[0] - 1), 0)
    exp_map = lambda t, te, nu: (te[t], 0, 0)
    grid_spec = pltpu.PrefetchScalarGridSpec(
        num_scalar_prefetch=2,
        grid=(n_tiles,),
        in_specs=[pl.BlockSpec((TM_FFN * SUBLANES, LANES), row_map),
                  pl.BlockSpec((1, d, f2), exp_map), pl.BlockSpec((1, 1, f2), exp_map),
                  pl.BlockSpec((1, f2 // 2, d), exp_map), pl.BlockSpec((1, 1, d), exp_map)],
        out_specs=pl.BlockSpec((TM_FFN * SUBLANES, LANES), lambda t, te, nu: (t, 0)),
    )
    ys2 = pl.pallas_call(
        _moe_ffn_kernel,
        out_shape=jax.ShapeDtypeStruct((n_rows * SUBLANES, LANES), F32),
        grid_spec=grid_spec,
        compiler_params=_params("arbitrary"),
        name="moe_ffn",
    )(tile_expert, n_used, hs3.reshape(n_rows * SUBLANES, LANES), w1, b1, w2, b2)
    return ys2.reshape(hs3.shape)


def _combine_kernel(pos_ref, p_ref, ys_hbm, x_ref, mod_ref, x_o, ybuf, abuf, sem):
    def issue(jj, carry):
        for kk in range(TOP_K):
            pltpu.make_async_copy(ys_hbm.at[pos_ref[jj * TOP_K + kk]], ybuf.at[kk * TM + jj], sem).start()
        return carry

    lax.fori_loop(0, TM, issue, 0)
    n = TM * TOP_K
    pltpu.make_async_copy(ys_hbm.at[pl.ds(0, n)], ybuf, sem).wait()

    def mix(jj, carry):
        acc = p_ref[jj * TOP_K] * ybuf[jj]
        for kk in range(1, TOP_K):
            acc = acc + p_ref[jj * TOP_K + kk] * ybuf[kk * TM + jj]
        abuf[pl.ds(pl.multiple_of(jj * SUBLANES, SUBLANES), SUBLANES), :] = acc
        return carry

    lax.fori_loop(0, TM, mix, 0, unroll=8)
    mixed = jnp.concatenate([abuf[pl.ds(s, TM, stride=SUBLANES), :] for s in range(SUBLANES)], axis=1)
    x_o[0] = x_ref[0] + mod_ref[0, 0][5:6] * mixed


def _combine(pos_flat, p_flat, ys3, x, mod):
    bsz, tu, d = x.shape
    nt = tu // TM
    smem_spec = pl.BlockSpec((TM * TOP_K,), lambda b, i: (b * nt + i,), memory_space=pltpu.SMEM)
    return pl.pallas_call(
        _combine_kernel,
        out_shape=jax.ShapeDtypeStruct(x.shape, F32),
        grid=(bsz, nt),
        in_specs=[smem_spec, smem_spec, pl.BlockSpec(memory_space=pl.ANY), _row_spec(d), _mod_spec(d)],
        out_specs=_row_spec(d),
        scratch_shapes=[pltpu.VMEM((TM * TOP_K, SUBLANES, LANES), F32), pltpu.VMEM((TM * SUBLANES, LANES), F32),
                        pltpu.SemaphoreType.DMA(())],
        compiler_params=_params("arbitrary", "arbitrary"),
        name="moe_combine",
    )(pos_flat, p_flat, ys3, x, mod)


def _moe(x, hf, idx_pad, p_pad, mod, w1, b1, w2, b2):
    bsz, tu, d = x.shape
    n_tok = bsz * tu
    n_rows = n_tok * TOP_K + N_EXPERTS * TM_FFN
    n_tiles = n_rows // TM_FFN
    idx = idx_pad.reshape(n_tok, LANES)[:, :TOP_K]
    onehot = jnp.sum((idx[:, :, None] == jnp.arange(N_EXPERTS, dtype=jnp.int32)).astype(jnp.int32), axis=1)
    csum = jnp.cumsum(onehot, axis=0)
    counts = csum[-1]
    padded = ((counts + TM_FFN - 1) // TM_FFN) * TM_FFN
    ends = jnp.cumsum(padded)
    pos = (ends - padded)[idx] + jnp.take_along_axis(csum - onehot, idx, axis=1)
    n_used = (ends[-1] // TM_FFN).astype(jnp.int32)
    tile_start = jnp.arange(n_tiles, dtype=jnp.int32) * TM_FFN
    tile_expert = jnp.sum((tile_start[:, None] >= ends[None, :]).astype(jnp.int32), axis=1)
    last = jnp.take(tile_expert, jnp.maximum(n_used - 1, 0))
    tile_expert = jnp.where(jnp.arange(n_tiles) < n_used, tile_expert, last).astype(jnp.int32)
    pos_flat = pos.reshape(-1).astype(jnp.int32)

    hs3 = _dispatch(pos_flat, hf.reshape(n_tok, d), n_rows)
    ys3 = _moe_ffn(tile_expert, n_used.reshape(1), hs3, w1, b1, w2, b2)
    p_flat = p_pad.reshape(n_tok, LANES)[:, :TOP_K].reshape(-1)
    return _combine(pos_flat, p_flat, ys3, x, mod)


def _rope_tables(n_ctx, n_lat):
    def angles(rot_dim):
        rows = n_lat // GRID_W
        row = jnp.repeat(jnp.arange(rows, dtype=F32), GRID_W)
        col = jnp.tile(jnp.arange(GRID_W, dtype=F32), rows)
        half = rot_dim // 2
        inv_freq = ROPE_THETA ** (-jnp.arange(0, half, 2, dtype=F32) / half)
        return row[:, None] * inv_freq[None, :], col[:, None] * inv_freq[None, :]

    def with_ctx(t, fill):
        return jnp.concatenate([jnp.full((n_ctx, LANES), fill, F32), t], axis=0)

    ar, ac = angles(MLA_ROPE)
    one = jnp.ones((n_lat, 32), F32)
    zero = jnp.zeros((n_lat, 32), F32)
    cm = jnp.concatenate([jnp.cos(ar), jnp.cos(ac), one, jnp.cos(ar), jnp.cos(ac), one], axis=1)
    sm = jnp.concatenate([-jnp.sin(ar), -jnp.sin(ac), zero, jnp.sin(ar), jnp.sin(ac), zero], axis=1)
    ar, ac = angles(GQA_HEAD_DIM)
    cg = jnp.concatenate([jnp.cos(ar), jnp.cos(ac), jnp.cos(ar), jnp.cos(ac)], axis=1)
    sg = jnp.concatenate([-jnp.sin(ar), -jnp.sin(ac), jnp.sin(ar), jnp.sin(ac)], axis=1)
    return with_ctx(cm, 1.0), with_ctx(sm, 0.0), with_ctx(cg, 1.0), with_ctx(sg, 0.0)


_PERM_GQA = np.concatenate([np.arange(0, 32), np.arange(64, 96), np.arange(32, 64), np.arange(96, 128)])
_SRC_MLA = np.zeros(LANES, np.int32)
_MASK_MLA = np.zeros(LANES, np.float32)
_SRC_MLA[0:16], _SRC_MLA[16:32], _SRC_MLA[64:80], _SRC_MLA[80:96] = (
    np.arange(0, 16), np.arange(32, 48), np.arange(16, 32), np.arange(48, 64))
_MASK_MLA[0:32] = 1.0
_MASK_MLA[64:96] = 1.0


def _pad_rope64(w):
    return jnp.take(w, _SRC_MLA, axis=-1) * _MASK_MLA


def _attn_weights(e, w_in, q_norm, w_uq, kv_norm, w_ukv, qn_g, qr_g, kn_g, kr_g, gq_g, gk_g, w_out):
    s1 = MLA_Q_RANK
    s2 = s1 + MLA_KV_RANK
    s3 = s2 + MLA_ROPE
    s4 = s3 + GQA_HEADS * GQA_HEAD_DIM
    s5 = s4 + GQA_KV_HEADS * GQA_HEAD_DIM
    wi = w_in[e]
    d = wi.shape[0]
    gq = wi[:, s3:s4].reshape(d, GQA_HEADS, GQA_HEAD_DIM)[:, :, _PERM_GQA].reshape(d, -1)
    gk = wi[:, s4:s5].reshape(d, GQA_KV_HEADS, GQA_HEAD_DIM)[:, :, _PERM_GQA].reshape(d, -1)
    w_in_p = jnp.concatenate([wi[:, :s2], _pad_rope64(wi[:, s2:s3]), gq, gk, wi[:, s5:]], axis=1)
    uq = w_uq[e].reshape(MLA_Q_RANK, MLA_HEADS, MLA_NOPE + MLA_ROPE)
    uq_p = jnp.concatenate([uq[:, :, :MLA_NOPE], _pad_rope64(uq[:, :, MLA_NOPE:])], axis=-1)
    row = lambda g: g.reshape(1, -1).astype(F32)
    return dict(
        w_in=w_in_p.astype(BF16), q_norm=row(q_norm[e]), w_uq=uq_p.reshape(MLA_Q_RANK, -1).astype(BF16),
        kv_norm=row(kv_norm[e]), w_ukv=w_ukv[e].astype(BF16),
        qn_g=row(qn_g[e]), qr_g=row(_pad_rope64(qr_g[e])), kn_g=row(kn_g[e]), kr_g=row(_pad_rope64(kr_g[e])),
        gq_g=row(gq_g[e][_PERM_GQA]), gk_g=row(gk_g[e][_PERM_GQA]), w_out=w_out[e].astype(BF16))


def _pad_cols(w, n):
    return jnp.pad(w, ((0, 0), (0, n - w.shape[1])))


def _pad_rows(w, n):
    return jnp.pad(w, ((0, n - w.shape[0]), (0, 0)))


def _lora_pair(w1, w2):
    rank = w1.shape[-1]
    down = jnp.concatenate([w1[0], w1[1]], axis=1)
    up = jnp.stack([jnp.pad(w2[0], ((0, rank), (0, 0))), jnp.pad(w2[1], ((rank, 0), (0, 0)))])
    return down.astype(BF16), up.astype(BF16)


def _rwkv_weights(j, mix, w_r, w_k, w_v, w_o, w0, w1, w2, a0, a1, a2, v0, v1, v2, g1, g2, k_k, k_a, r_k, ln_w, ln_b):
    d = w_r.shape[1]
    row = lambda g: g.reshape(1, -1).astype(F32)
    heads = np.arange(d) // RWKV_HEAD
    hs1 = (heads[:, None] == np.arange(LANES)[None, :]).astype(np.float32)
    w1c, w2p = _lora_pair(w1[j], w2[j])
    a1c, a2p = _lora_pair(a1[j], a2[j])
    rp = dict(mix=mix[j], w_r=w_r[j].astype(BF16), w_k=w_k[j].astype(BF16), w_v=w_v[j].astype(BF16),
              w_o=w_o[j].astype(BF16), w0=w0[j], w1=w1c, w2=w2p, a0=a0[j], a1=a1c, a2=a2p,
              g1=g1[j].astype(BF16), g2=g2[j].astype(BF16), k_k=row(k_k[j]), k_a=row(k_a[j]), r_k=row(r_k[j]),
              ln_w=row(ln_w[j]), ln_b=row(ln_b[j]),
              hs1=jnp.asarray(hs1, BF16), hs2=jnp.asarray(hs1.T, BF16))
    if j > 0:
        rp.update(v0=row(v0[j - 1]), v1=_pad_cols(v1[j - 1], LANES).astype(BF16),
                  v2=_pad_rows(v2[j - 1], LANES).astype(BF16))
    return rp


def kernel(x, c, ctx, c_ctx, ada_w, ada_b, norm_mix, norm_ffn, attn_w_in, mla_q_norm, mla_w_uq, mla_kv_norm, mla_w_ukv, mla_qn_g, mla_qr_g, mla_kn_g, mla_kr_g, gqa_q_g, gqa_k_g, attn_w_out, rwkv_mix, rwkv_w_r, rwkv_w_k, rwkv_w_v, rwkv_w_o, rwkv_w0, rwkv_w1, rwkv_w2, rwkv_a0, rwkv_a1, rwkv_a2, rwkv_v0, rwkv_v1, rwkv_v2, rwkv_g1, rwkv_g2, rwkv_k_k, rwkv_k_a, rwkv_r_k, rwkv_ln_w, rwkv_ln_b, moe_router_w, moe_router_b, moe_w1, moe_b1, moe_w2, moe_b2):
    bsz, n_lat, d = x.shape
    n_ctx = ctx.shape[1]
    depth = ada_w.shape[0]
    tu = n_ctx + n_lat
    assert n_ctx % TM == 0 and n_lat % TM == 0 and n_ctx % WKV_CHUNK == 0 and d == SUBLANES * LANES
    assert (bsz * tu * TOP_K) % TM_FFN == 0

    rows = -(-(bsz + 1) // SUBLANES) * SUBLANES
    cond = jnp.zeros((rows, d), F32).at[:bsz].set(c).at[bsz].set(c_ctx)
    mods = _adaln(cond, ada_w, ada_b)
    mods_lat = mods[:, :bsz].reshape(depth, bsz, 1, N_MOD, d)
    mods_ctx = jnp.broadcast_to(mods[:, bsz].reshape(depth, 1, 1, N_MOD, d), mods_lat.shape)
    mods = jnp.concatenate([mods_ctx, mods_lat], axis=2)

    tabs = _rope_tables(n_ctx, n_lat)
    xs = jnp.concatenate([ctx, x], axis=1)

    n_e, _, f2 = moe_w1.shape[1:]
    rw_all = jnp.pad(moe_router_w, ((0, 0), (0, 0), (0, LANES - n_e)))
    rb_all = jnp.pad(moe_router_b, ((0, 0), (0, LANES - n_e)), constant_values=-1e30).reshape(depth, 1, LANES)

    w1_all = _deinterleave_w1(moe_w1)

    v_first = None
    for i in range(depth):
        mod = mods[i]
        nm = norm_mix[i].reshape(1, d)
        nf = norm_ffn[i].reshape(1, d)
        rw, rb = rw_all[i], rb_all[i]
        if i % 2 == 0:
            aw = _attn_weights(i // 2, attn_w_in, mla_q_norm, mla_w_uq, mla_kv_norm, mla_w_ukv, mla_qn_g,
                               mla_qr_g, mla_kn_g, mla_kr_g, gqa_q_g, gqa_k_g, attn_w_out)
            qm, km, vm, qg, kg, vg = _attn_proj(xs, mod, nm, aw, tabs)
            om = _attention(qm, km, vm, n_ctx)
            og = _attention(qg, kg, vg, n_ctx)
            xs, hf, idx_pad, p_pad = _attn_out(om, og, xs, mod, aw["w_out"], nf, rw, rb)
        else:
            j = i // 2
            rp = _rwkv_weights(j, rwkv_mix, rwkv_w_r, rwkv_w_k, rwkv_w_v, rwkv_w_o, rwkv_w0, rwkv_w1, rwkv_w2,
                               rwkv_a0, rwkv_a1, rwkv_a2, rwkv_v0, rwkv_v1, rwkv_v2, rwkv_g1, rwkv_g2,
                               rwkv_k_k, rwkv_k_a, rwkv_r_k, rwkv_ln_w, rwkv_ln_b)
            r, v, kk, w, k, b, g, bv = _rwkv_feat(xs, mod, nm, rp, v_first if j > 0 else None)
            if j == 0:
                v_first = v
            y = _wkv_scan(r, v, kk, w, k, b, n_ctx)
            xs, hf, idx_pad, p_pad = _rwkv_out(y, g, bv, xs, mod, rp, nf, rw, rb)
        w1 = w1_all[i]
        b1 = jnp.concatenate([moe_b1[i][:, 0::2], moe_b1[i][:, 1::2]], axis=-1).reshape(n_e, 1, f2)
        w2 = moe_w2[i].astype(BF16)
        b2 = moe_b2[i].reshape(n_e, 1, d)
        xs = _moe(xs, hf, idx_pad, p_pad, mod, w1, b1, w2, b2)
    return xs[:, n_ctx:]
```

```python
import functools

import jax
import jax.numpy as jnp
import numpy as np
from jax import lax
from jax.experimental import pallas as pl
from jax.experimental.pallas import tpu as pltpu

F32 = jnp.float32
BF16 = jnp.bfloat16
HIGHEST = lax.Precision.HIGHEST

SUBLANES = 8
LANES = 128
VMEM_LIMIT = 56 * 1024 * 1024

GRID_W = 64
ROPE_THETA = 10000.0
NORM_EPS = 1e-6
N_MOD = 6

MLA_HEADS = 4
MLA_Q_RANK = 384
MLA_KV_RANK = 256
MLA_NOPE = 128
MLA_ROPE = 64
MLA_V = 128
MLA_SCALE = (MLA_NOPE + MLA_ROPE) ** -0.5
GQA_HEADS = 4
GQA_KV_HEADS = 2
GQA_HEAD_DIM = 128
GQA_SCALE = GQA_HEAD_DIM ** -0.5
LOG2E = float(np.log2(np.e))

RWKV_HEAD = 64
RWKV_GN_EPS = 64e-5
N_MIX = 6
DECAY_SCALE = float(np.exp(-0.5))

N_EXPERTS = 32
TOP_K = 4
SWIGLU_ALPHA = 1.702
SWIGLU_LIMIT = 7.0

TM = 256
TM_FFN = 512
WKV_CHUNK = 64
WKV_ROWS = 4


def _params(*sem):
    return pltpu.CompilerParams(dimension_semantics=sem, vmem_limit_bytes=VMEM_LIMIT)


def _dot(a, b):
    return jnp.dot(a, b, preferred_element_type=F32)


def _bdot(a, b_ref):
    return jnp.dot(a.astype(BF16), b_ref[...], preferred_element_type=F32)


def _rms(x, g, n):
    ms = jnp.sum(x * x, axis=-1, keepdims=True) * (1.0 / n)
    return x * lax.rsqrt(ms + NORM_EPS) * g


def _sigmoid(x):
    return 1.0 / (1.0 + jnp.exp(-x))


def _rope(x, cos, sin):
    return x * cos + pltpu.roll(x, 64, axis=1) * sin


def _adaln_kernel(c_ref, w_ref, b_ref, o_ref):
    c = c_ref[...]
    s = c * _sigmoid(c)
    o_ref[0] = jnp.dot(s, w_ref[0], precision=HIGHEST, preferred_element_type=F32) + b_ref[0]


def _adaln(cond, ada_w, ada_b):
    depth, d, nd = ada_w.shape
    rows = cond.shape[0]
    return pl.pallas_call(
        _adaln_kernel,
        out_shape=jax.ShapeDtypeStruct((depth, rows, nd), F32),
        grid=(depth, nd // d),
        in_specs=[pl.BlockSpec((rows, d), lambda l, n: (0, 0)),
                  pl.BlockSpec((1, d, d), lambda l, n: (l, 0, n)),
                  pl.BlockSpec((1, 1, d), lambda l, n: (l, 0, n))],
        out_specs=pl.BlockSpec((1, rows, d), lambda l, n: (l, 0, n)),
        compiler_params=_params("arbitrary", "arbitrary"),
        name="adaln",
    )(cond, ada_w, ada_b.reshape(depth, 1, nd))


def _ffn_tail(x2, mod, nf_ref, rw_ref, rb_ref, hf_o, idx_o, p_o):
    d = x2.shape[-1]
    hf = _rms(x2, nf_ref[...], d) * (1.0 + mod[4:5]) + mod[3:4]
    hf_o[0] = hf
    logits = jnp.dot(hf, rw_ref[...], precision=HIGHEST, preferred_element_type=F32) + rb_ref[...]
    lane = lax.broadcasted_iota(jnp.int32, logits.shape, 1).astype(F32)
    vals, idxs = [], []
    l = logits
    for _ in range(TOP_K):
        m = jnp.max(l, axis=-1, keepdims=True)
        ik = jnp.min(jnp.where(l == m, lane, float(LANES)), axis=-1, keepdims=True)
        vals.append(m)
        idxs.append(ik)
        l = jnp.where(lane == ik, -jnp.inf, l)
    es = [jnp.exp(v - vals[0]) for v in vals]
    inv = 1.0 / (es[0] + es[1] + es[2] + es[3])
    idx_acc = jnp.zeros_like(logits)
    p_acc = jnp.zeros_like(logits)
    for k in range(TOP_K):
        idx_acc = jnp.where(lane == float(k), idxs[k], idx_acc)
        p_acc = jnp.where(lane == float(k), es[k] * inv, p_acc)
    idx_o[0] = idx_acc.astype(jnp.int32)
    p_o[0] = p_acc


def _attn_proj_kernel(x_ref, mod_ref, nm_ref, win_ref, qn_ref, wuq_ref, kvn_ref, wukv_ref,
                      qng_ref, qrg_ref, kng_ref, krg_ref, gqg_ref, gkg_ref,
                      cm_ref, sm_ref, cg_ref, sg_ref,
                      qm_o, km_o, vm_o, qg_o, kg_o, vg_o):
    x = x_ref[0]
    mod = mod_ref[0, 0]
    d = x.shape[-1]
    h = _rms(x, nm_ref[...], d) * (1.0 + mod[1:2]) + mod[0:1]
    z = _bdot(h, win_ref)
    o1 = MLA_Q_RANK
    o2 = o1 + MLA_KV_RANK
    o3 = o2 + LANES
    o4 = o3 + GQA_HEADS * GQA_HEAD_DIM
    o5 = o4 + GQA_KV_HEADS * GQA_HEAD_DIM
    q = _bdot(_rms(z[:, :o1], qn_ref[...], MLA_Q_RANK), wuq_ref)
    kv = _bdot(_rms(z[:, o1:o2], kvn_ref[...], MLA_KV_RANK), wukv_ref)
    cm, sm, cg, sg = cm_ref[...], sm_ref[...], cg_ref[...], sg_ref[...]
    kr = _rope(_rms(z[:, o2:o3], krg_ref[...], MLA_ROPE), cm, sm)
    for hh in range(MLA_HEADS):
        b0 = hh * 2 * LANES
        qn = _rms(q[:, b0:b0 + LANES], qng_ref[...], MLA_NOPE)
        qr = _rope(_rms(q[:, b0 + LANES:b0 + 2 * LANES], qrg_ref[...], MLA_ROPE), cm, sm)
        qm_o[0, hh] = (jnp.concatenate([qn, qr], axis=1) * (MLA_SCALE * LOG2E)).astype(BF16)
        kn = _rms(kv[:, b0:b0 + LANES], kng_ref[...], MLA_NOPE)
        km_o[0, hh] = jnp.concatenate([kn, kr], axis=1).astype(BF16)
        vm_o[0, hh] = kv[:, b0 + LANES:b0 + 2 * LANES].astype(BF16)
    for hh in range(GQA_HEADS):
        g = z[:, o3 + hh * LANES:o3 + (hh + 1) * LANES]
        qg_o[0, hh] = (_rope(_rms(g, gqg_ref[...], GQA_HEAD_DIM), cg, sg) * (GQA_SCALE * LOG2E)).astype(BF16)
    for hh in range(GQA_KV_HEADS):
        g = z[:, o4 + hh * LANES:o4 + (hh + 1) * LANES]
        kg_o[0, hh] = _rope(_rms(g, gkg_ref[...], GQA_HEAD_DIM), cg, sg).astype(BF16)
        vg_o[0, hh] = z[:, o5 + hh * LANES:o5 + (hh + 1) * LANES].astype(BF16)


def _full(shape):
    nd = len(shape)
    return pl.BlockSpec(shape, lambda b, i: (0,) * nd)


def _mod_spec(d):
    return pl.BlockSpec((1, 1, N_MOD, d), lambda b, i: (b, jnp.minimum(i, 1), 0, 0))


def _row_spec(d):
    return pl.BlockSpec((1, TM, d), lambda b, i: (b, i, 0))


def _attn_proj(x, mod, nm, aw, tabs):
    bsz, tu, d = x.shape
    nt = tu // TM
    head_spec = lambda nh, w: pl.BlockSpec((1, nh, TM, w), lambda b, i: (b, 0, i, 0))
    tab_spec = pl.BlockSpec((TM, LANES), lambda b, i: (i, 0))
    weights = [nm, aw["w_in"], aw["q_norm"], aw["w_uq"], aw["kv_norm"], aw["w_ukv"],
               aw["qn_g"], aw["qr_g"], aw["kn_g"], aw["kr_g"], aw["gq_g"], aw["gk_g"]]
    out_shape = [
        jax.ShapeDtypeStruct((bsz, MLA_HEADS, tu, 2 * LANES), BF16),
        jax.ShapeDtypeStruct((bsz, MLA_HEADS, tu, 2 * LANES), BF16),
        jax.ShapeDtypeStruct((bsz, MLA_HEADS, tu, LANES), BF16),
        jax.ShapeDtypeStruct((bsz, GQA_HEADS, tu, LANES), BF16),
        jax.ShapeDtypeStruct((bsz, GQA_KV_HEADS, tu, LANES), BF16),
        jax.ShapeDtypeStruct((bsz, GQA_KV_HEADS, tu, LANES), BF16),
    ]
    return pl.pallas_call(
        _attn_proj_kernel,
        out_shape=out_shape,
        grid=(bsz, nt),
        in_specs=[_row_spec(d), _mod_spec(d)] + [_full(w.shape) for w in weights] + [tab_spec] * 4,
        out_specs=[head_spec(MLA_HEADS, 2 * LANES), head_spec(MLA_HEADS, 2 * LANES),
                   head_spec(MLA_HEADS, LANES), head_spec(GQA_HEADS, LANES),
                   head_spec(GQA_KV_HEADS, LANES), head_spec(GQA_KV_HEADS, LANES)],
        compiler_params=_params("parallel", "parallel"),
        name="attn_proj",
    )(x, mod, *weights, *tabs)


def _attn_kernel(n_ctx, q_ref, k_ref, v_ref, o_ref):
    q = q_ref[0, 0]

    def attend(k, v):
        s = lax.dot_general(q, k, (((1,), (1,)), ((), ())), preferred_element_type=F32)
        m = jnp.max(s, axis=-1, keepdims=True)
        p = jnp.exp2(s - m)
        l = jnp.sum(p, axis=-1, keepdims=True)
        o = _dot(p.astype(BF16), v)
        o_ref[0] = (o * (1.0 / l)).astype(o_ref.dtype)

    is_ctx = pl.program_id(2) == 0

    @pl.when(is_ctx)
    def _():
        attend(k_ref[0, 0, :n_ctx], v_ref[0, 0, :n_ctx])

    @pl.when(jnp.logical_not(is_ctx))
    def _():
        attend(k_ref[0, 0], v_ref[0, 0])


def _attention(q, k, v, n_ctx):
    bsz, hq, tu, dk = q.shape
    hk = k.shape[1]
    grp = hq // hk
    dv = v.shape[-1]
    return pl.pallas_call(
        functools.partial(_attn_kernel, n_ctx),
        out_shape=jax.ShapeDtypeStruct((bsz, tu, hq * dv), BF16),
        grid=(bsz, hq, tu // TM),
        in_specs=[pl.BlockSpec((1, 1, TM, dk), lambda b, h, i: (b, h, i, 0)),
                  pl.BlockSpec((1, 1, tu, dk), lambda b, h, i: (b, h // grp, 0, 0)),
                  pl.BlockSpec((1, 1, tu, dv), lambda b, h, i: (b, h // grp, 0, 0))],
        out_specs=pl.BlockSpec((1, TM, dv), lambda b, h, i: (b, i, h)),
        compiler_params=_params("parallel", "parallel", "arbitrary"),
        name="attention",
    )(q, k, v)


def _attn_out_kernel(om_ref, og_ref, x_ref, mod_ref, wo_ref, nf_ref, rw_ref, rb_ref,
                     x_o, hf_o, idx_o, p_o):
    mod = mod_ref[0, 0]
    nm = om_ref.shape[-1]
    o = _dot(om_ref[0], wo_ref[:nm]) + _dot(og_ref[0], wo_ref[nm:])
    x2 = x_ref[0] + mod[2:3] * o
    x_o[0] = x2
    _ffn_tail(x2, mod, nf_ref, rw_ref, rb_ref, hf_o, idx_o, p_o)


def _tail_out_shapes(bsz, tu, d):
    return [jax.ShapeDtypeStruct((bsz, tu, d), F32), jax.ShapeDtypeStruct((bsz, tu, d), F32),
            jax.ShapeDtypeStruct((bsz, tu, LANES), jnp.int32), jax.ShapeDtypeStruct((bsz, tu, LANES), F32)]


def _tail_out_specs(d):
    return [_row_spec(d), _row_spec(d), _row_spec(LANES), _row_spec(LANES)]


def _attn_out(om, og, x, mod, wo, nf, rw, rb):
    bsz, tu, d = x.shape
    return pl.pallas_call(
        _attn_out_kernel,
        out_shape=_tail_out_shapes(bsz, tu, d),
        grid=(bsz, tu // TM),
        in_specs=[_row_spec(om.shape[-1]), _row_spec(og.shape[-1]), _row_spec(d), _mod_spec(d),
                  _full(wo.shape), _full(nf.shape), _full(rw.shape), _full(rb.shape)],
        out_specs=_tail_out_specs(d),
        compiler_params=_params("parallel", "parallel"),
        name="attn_out",
    )(om, og, x, mod, wo, nf, rw, rb)


def _head_sum(x, hs1_ref, hs2_ref):
    hi = x.astype(BF16)
    lo = (x - hi.astype(F32)).astype(BF16)
    s = _dot(hi, hs1_ref[...]) + _dot(lo, hs1_ref[...])
    shi = s.astype(BF16)
    slo = (s - shi.astype(F32)).astype(BF16)
    return _dot(shi, hs2_ref[...]) + _dot(slo, hs2_ref[...])


def _rwkv_feat_kernel(has_vres, nt, x_ref, xp_ref, xn_ref, mod_ref, nm_ref, mix_ref,
                      wr_ref, wk_ref, wv_ref, w0_ref, w1_ref, w2_ref, a0_ref, a1_ref, a2_ref,
                      g1_ref, g2_ref, kk_ref, ka_ref, rk_ref, hs1_ref, hs2_ref, *rest):
    if has_vres:
        v0_ref, v1_ref, v2_ref, vf_ref = rest[:4]
        rest = rest[4:]
    r_o, v_o, kk_o, w_o, k_o, b_o, g_o, bv_o = rest
    i = pl.program_id(1)
    mod = mod_ref[0, 0]
    d = x_ref.shape[-1]
    nm = nm_ref[...]

    def hmod(xx):
        return _rms(xx, nm, d) * (1.0 + mod[1:2]) + mod[0:1]

    h = hmod(x_ref[0])
    has_prev = (i >= 2).astype(F32)
    has_next = jnp.logical_and(i >= 1, i < nt - 1).astype(F32)
    prev_row = hmod(xp_ref[0])[SUBLANES - 1:SUBLANES] * has_prev
    next_row = hmod(xn_ref[0])[0:1] * has_next
    row = lax.broadcasted_iota(jnp.int32, h.shape, 0)
    hp = jnp.where(row == 0, prev_row, pltpu.roll(h, 1, axis=0))
    hn = jnp.where(row == TM - 1, next_row, pltpu.roll(h, TM - 1, axis=0))
    xx = 0.5 * (hp + hn) - h
    xr, xw, xk, xv, xa, xg = (h + xx * mix_ref[m:m + 1] for m in range(N_MIX))

    r = _bdot(xr, wr_ref)
    k = _bdot(xk, wk_ref)
    v = _bdot(xv, wv_ref)
    if has_vres:
        gate = _sigmoid(v0_ref[...] + _bdot(_bdot(xv, v1_ref), v2_ref))
        v = v + (vf_ref[0] - v) * gate
    kk = k * kk_ref[...]
    kk = kk * lax.rsqrt(jnp.maximum(_head_sum(kk * kk, hs1_ref, hs2_ref), 1e-24))
    tw = jnp.tanh(_bdot(xw, w1_ref)).astype(BF16)
    ta = _bdot(xa, a1_ref).astype(BF16)
    ksum = jnp.zeros_like(k)
    for dd in range(2):
        z = w0_ref[dd:dd + 1] + _dot(tw, w2_ref[dd])
        w_o[dd, 0] = -DECAY_SCALE * _sigmoid(z)
        a = _sigmoid(a0_ref[dd:dd + 1] + _dot(ta, a2_ref[dd]))
        kd = k * (1.0 + (a - 1.0) * ka_ref[...])
        k_o[dd, 0] = kd
        b_o[dd, 0] = kk * a
        ksum = ksum + kd
    g_o[0] = _bdot(_sigmoid(_bdot(xg, g1_ref)), g2_ref)
    bv_o[0] = _head_sum(r * ksum * rk_ref[...], hs1_ref, hs2_ref) * v
    r_o[0] = r
    v_o[0] = v
    kk_o[0] = kk


def _rwkv_feat(x, mod, nm, rp, v_first):
    bsz, tu, d = x.shape
    nt = tu // TM
    has_vres = v_first is not None
    nblk8 = tu // SUBLANES
    per8 = TM // SUBLANES
    prev_spec = pl.BlockSpec((1, SUBLANES, d), lambda b, i: (b, jnp.maximum(i * per8 - 1, 0), 0))
    next_spec = pl.BlockSpec((1, SUBLANES, d), lambda b, i: (b, jnp.minimum((i + 1) * per8, nblk8 - 1), 0))
    weights = [nm, rp["mix"], rp["w_r"], rp["w_k"], rp["w_v"], rp["w0"], rp["w1"], rp["w2"],
               rp["a0"], rp["a1"], rp["a2"], rp["g1"], rp["g2"], rp["k_k"], rp["k_a"], rp["r_k"],
               rp["hs1"], rp["hs2"]]
    extra, extra_specs = [], []
    if has_vres:
        extra = [rp["v0"], rp["v1"], rp["v2"], v_first]
        extra_specs = [_full(rp["v0"].shape), _full(rp["v1"].shape), _full(rp["v2"].shape), _row_spec(d)]
    one = jax.ShapeDtypeStruct((bsz, tu, d), F32)
    two = jax.ShapeDtypeStruct((2, bsz, tu, d), F32)
    dir_spec = pl.BlockSpec((2, 1, TM, d), lambda b, i: (0, b, i, 0))
    return pl.pallas_call(
        functools.partial(_rwkv_feat_kernel, has_vres, nt),
        out_shape=[one, one, one, two, two, two, one, one],
        grid=(bsz, nt),
        in_specs=[_row_spec(d), prev_spec, next_spec, _mod_spec(d)] + [_full(w.shape) for w in weights] + extra_specs,
        out_specs=[_row_spec(d)] * 3 + [dir_spec] * 3 + [_row_spec(d)] * 2,
        compiler_params=_params("parallel", "parallel"),
        name="rwkv_feat",
    )(x, x, x, mod, *weights, *extra)


def _dot_nt(a, b):
    return lax.dot_general(a, b, (((1,), (1,)), ((), ())), preferred_element_type=F32)


def _wkv_kernel(r_ref, v_ref, kk_ref, lw_ref, k_ref, b_ref, cum_ref, msk_ref, y_ref, st_scr):
    @pl.when(pl.program_id(2) == 0)
    def _():
        st_scr[...] = jnp.zeros_like(st_scr)

    n_rows = r_ref.shape[0]
    n_pairs = r_ref.shape[-1] // LANES
    cum = cum_ref[0]
    at, rt, kt, bt, vs, lam_tot = [], [], [], [], [], []
    for bi in range(n_rows):
        lw = lw_ref[0, bi]
        lw_hi = lw.astype(BF16)
        lw_lo = (lw - lw_hi.astype(F32)).astype(BF16)
        cs = _dot(cum, lw_hi) + _dot(cum, lw_lo)
        lam_inv = jnp.exp(-cs)
        rt.append(r_ref[bi] * jnp.exp(cs))
        at.append(-kk_ref[bi] * jnp.exp(cs - lw))
        kt.append(k_ref[0, bi] * lam_inv)
        bt.append(b_ref[0, bi] * lam_inv)
        vs.append(v_ref[bi])
        lam_tot.append(jnp.exp(jnp.sum(lw, axis=0, keepdims=True)))

    strict = msk_ref[0, 0][:, :WKV_CHUNK] > 0.5
    m_ak = msk_ref[0, 1] > 0.5
    m_y = msk_ref[0, 2] > 0.5
    head_of_lane = lax.broadcasted_iota(jnp.int32, (1, LANES), 1) // RWKV_HEAD
    rowi = lax.broadcasted_iota(jnp.int32, (LANES, LANES), 0)
    coli = lax.broadcasted_iota(jnp.int32, (LANES, LANES), 1)
    blockdiag = (rowi // RWKV_HEAD) == (coli // RWKV_HEAD)

    hpp = LANES // RWKV_HEAD
    n_sq = WKV_CHUNK.bit_length() - 1
    pairs = [(bi, p) for bi in range(n_rows) for p in range(n_pairs)]
    heads = [(q, hh) for q in pairs for hh in range(hpp)]
    sls = {q: slice(q[1] * LANES, (q[1] + 1) * LANES) for q in pairs}
    ar = {q: jnp.concatenate([at[q[0]][:, sls[q]], rt[q[0]][:, sls[q]]], axis=0) for q in pairs}
    bk = {q: jnp.concatenate([bt[q[0]][:, sls[q]], kt[q[0]][:, sls[q]]], axis=0).astype(BF16) for q in pairs}
    vp = {q: vs[q[0]][:, sls[q]] for q in pairs}
    vv = {q: jnp.concatenate([vp[q], vp[q]], axis=0).astype(BF16) for q in pairs}
    st = {q: st_scr[q[0] * n_pairs + q[1]] for q in pairs}
    rhs = {q: jnp.concatenate([bk[q], st[q].astype(BF16)], axis=0) for q in pairs}
    arm = {h: jnp.where(head_of_lane == h[1], ar[h[0]], 0.0).astype(BF16) for h in heads}
    gs = {h: _dot_nt(arm[h], rhs[h[0]]) for h in heads}
    g = {h: gs[h][:, :2 * WKV_CHUNK] for h in heads}
    s0 = {h: gs[h][:, 2 * WKV_CHUNK:] for h in heads}
    u = {h: _dot(jnp.where(m_ak, g[h][:WKV_CHUNK], 0.0).astype(BF16), vv[h[0]]) + s0[h][:WKV_CHUNK]
         for h in heads}
    pk = {h: jnp.where(strict, g[h][:WKV_CHUNK, :WKV_CHUNK], 0.0) for h in heads}
    for it in range(n_sq):
        pkb = {h: pk[h].astype(BF16) for h in heads}
        u = {h: u[h] + _dot(pkb[h], u[h].astype(BF16)) for h in heads}
        if it < n_sq - 1:
            pk = {h: _dot(pkb[h], pkb[h]) for h in heads}
    ys = {h: _dot(jnp.where(m_y, g[h][WKV_CHUNK:], 0.0).astype(BF16),
                  jnp.concatenate([u[h], vp[h[0]]], axis=0).astype(BF16)) + s0[h][WKV_CHUNK:]
          for h in heads}
    for q in pairs:
        y_ref[0, q[0], :, sls[q]] = jnp.where(head_of_lane == 0, ys[(q, 0)], ys[(q, 1)])
        u_pair = jnp.where(head_of_lane == 0, u[(q, 0)], u[(q, 1)])
        uvp = jnp.concatenate([u_pair, vp[q]], axis=0).astype(BF16)
        upd = lax.dot_general(uvp, bk[q], (((0,), (0,)), ((), ())), preferred_element_type=F32)
        st_scr[q[0] * n_pairs + q[1]] = jnp.where(blockdiag, (st[q] + upd) * lam_tot[q[0]][:, sls[q]], 0.0)


def _scan_tables():
    t = np.arange(WKV_CHUNK)
    incl = [t[None, :] <= t[:, None], t[None, :] >= t[:, None]]
    strict = [t[None, :] < t[:, None], t[None, :] > t[:, None]]
    msk = np.zeros((2, 3, WKV_CHUNK, 2 * WKV_CHUNK), np.float32)
    for dd in range(2):
        msk[dd, 0, :, :WKV_CHUNK] = strict[dd]
        msk[dd, 1, :, WKV_CHUNK:] = strict[dd]
        msk[dd, 2, :, :WKV_CHUNK] = incl[dd]
        msk[dd, 2, :, WKV_CHUNK:] = incl[dd]
    return jnp.asarray(np.stack(incl).astype(np.float32), BF16), jnp.asarray(msk, F32)


def _wkv_scan(r, v, kk, lw, k, b, n_ctx):
    bsz, tu, d = r.shape
    nb = tu // WKV_CHUNK
    cb = n_ctx // WKV_CHUNK
    cum, msk = _scan_tables()

    def blk(dd, j):
        back = jnp.where(j < cb, cb - 1 - j, nb - 1 + cb - j)
        return jnp.where(dd == 0, j, back)

    rows = WKV_ROWS if bsz % WKV_ROWS == 0 else 1
    shared = pl.BlockSpec((rows, WKV_CHUNK, d), lambda dd, bb, j: (bb, blk(dd, j), 0))
    per_dir = pl.BlockSpec((1, rows, WKV_CHUNK, d), lambda dd, bb, j: (dd, bb, blk(dd, j), 0))
    return pl.pallas_call(
        _wkv_kernel,
        out_shape=jax.ShapeDtypeStruct((2, bsz, tu, d), F32),
        grid=(2, bsz // rows, nb),
        in_specs=[shared, shared, shared, per_dir, per_dir, per_dir,
                  pl.BlockSpec((1, WKV_CHUNK, WKV_CHUNK), lambda dd, bb, j: (dd, 0, 0)),
                  pl.BlockSpec((1, 3, WKV_CHUNK, 2 * WKV_CHUNK), lambda dd, bb, j: (dd, 0, 0, 0))],
        out_specs=per_dir,
        scratch_shapes=[pltpu.VMEM((rows * (d // LANES), LANES, LANES), F32)],
        compiler_params=_params("arbitrary", "arbitrary", "arbitrary"),
        name="wkv_scan",
    )(r, v, kk, lw, k, b, cum, msk)


def _rwkv_out_kernel(y_ref, g_ref, bv_ref, x_ref, mod_ref, lnw_ref, lnb_ref, wo_ref, hs1_ref, hs2_ref,
                     nf_ref, rw_ref, rb_ref, x_o, hf_o, idx_o, p_o):
    mod = mod_ref[0, 0]
    y = y_ref[0, 0] + y_ref[1, 0]
    inv_n = 1.0 / RWKV_HEAD
    mu = _head_sum(y, hs1_ref, hs2_ref) * inv_n
    dlt = y - mu
    var = _head_sum(dlt * dlt, hs1_ref, hs2_ref) * inv_n
    yn = dlt * lax.rsqrt(var + RWKV_GN_EPS) * lnw_ref[...] + lnb_ref[...]
    o = _bdot((yn + bv_ref[0]) * g_ref[0], wo_ref)
    x2 = x_ref[0] + mod[2:3] * o
    x_o[0] = x2
    _ffn_tail(x2, mod, nf_ref, rw_ref, rb_ref, hf_o, idx_o, p_o)


def _rwkv_out(y, g, bv, x, mod, rp, nf, rw, rb):
    bsz, tu, d = x.shape
    weights = [rp["ln_w"], rp["ln_b"], rp["w_o"], rp["hs1"], rp["hs2"], nf, rw, rb]
    return pl.pallas_call(
        _rwkv_out_kernel,
        out_shape=_tail_out_shapes(bsz, tu, d),
        grid=(bsz, tu // TM),
        in_specs=[pl.BlockSpec((2, 1, TM, d), lambda b, i: (0, b, i, 0))] + [_row_spec(d)] * 3 + [_mod_spec(d)]
        + [_full(w.shape) for w in weights],
        out_specs=_tail_out_specs(d),
        compiler_params=_params("parallel", "parallel"),
        name="rwkv_out",
    )(y, g, bv, x, mod, *weights)


def _dispatch_kernel(pos_ref, h_ref, init_hbm, out_hbm, hbuf, sem):
    del init_hbm
    h = h_ref[...]
    for s in range(SUBLANES):
        hbuf[:, s, :] = h[:, s * LANES:(s + 1) * LANES]

    def issue(jj, carry):
        for kk in range(TOP_K):
            pltpu.make_async_copy(hbuf.at[jj], out_hbm.at[pos_ref[jj * TOP_K + kk]], sem).start()
        return carry

    lax.fori_loop(0, TM, issue, 0, unroll=4)
    for _ in range(TOP_K):
        pltpu.make_async_copy(hbuf, out_hbm.at[pl.ds(0, TM)], sem).wait()


def _dispatch(pos_flat, h2, n_rows):
    n_tok, d = h2.shape
    init = jnp.zeros((n_rows, SUBLANES, LANES), h2.dtype)
    return pl.pallas_call(
        _dispatch_kernel,
        out_shape=jax.ShapeDtypeStruct(init.shape, init.dtype),
        grid=(n_tok // TM,),
        in_specs=[pl.BlockSpec((TM * TOP_K,), lambda i: (i,), memory_space=pltpu.SMEM),
                  pl.BlockSpec((TM, d), lambda i: (i, 0)), pl.BlockSpec(memory_space=pl.ANY)],
        out_specs=pl.BlockSpec(memory_space=pl.ANY),
        scratch_shapes=[pltpu.VMEM((TM, SUBLANES, LANES), h2.dtype), pltpu.SemaphoreType.DMA(())],
        input_output_aliases={2: 0},
        compiler_params=_params("arbitrary"),
        name="moe_dispatch",
    )(pos_flat, h2, init)


def _deinterleave_kernel(w_ref, p_ref, o_ref):
    half = o_ref.shape[-1] // 2
    for j in range(half // LANES):
        blk = _dot(w_ref[:, 2 * LANES * j:2 * LANES * (j + 1)].astype(BF16), p_ref[...]).astype(BF16)
        o_ref[:, LANES * j:LANES * (j + 1)] = blk[:, :LANES]
        o_ref[:, half + LANES * j:half + LANES * (j + 1)] = blk[:, LANES:]


def _deinterleave_w1(w1):
    f2 = w1.shape[-1]
    rows = int(np.prod(w1.shape[:-1]))
    col = jnp.arange(2 * LANES, dtype=jnp.int32)
    src = jnp.where(col < LANES, 2 * col, 2 * (col - LANES) + 1)
    perm = (col[:, None] == src[None, :]).astype(BF16)
    out = pl.pallas_call(
        _deinterleave_kernel,
        out_shape=jax.ShapeDtypeStruct((rows, f2), BF16),
        grid=(rows // TM_FFN,),
        in_specs=[pl.BlockSpec((TM_FFN, f2), lambda i: (i, 0)), pl.BlockSpec(perm.shape, lambda i: (0, 0))],
        out_specs=pl.BlockSpec((TM_FFN, f2), lambda i: (i, 0)),
        compiler_params=_params("parallel"),
        name="w1_deinterleave",
    )(w1.reshape(rows, f2), perm)
    return out.reshape(w1.shape)


def _moe_ffn_kernel(te_ref, nu_ref, hs_ref, w1_ref, b1_ref, w2_ref, b2_ref, ys_ref):
    t = pl.program_id(0)

    @pl.when(t < nu_ref[0])
    def _():
        x = jnp.concatenate([hs_ref[pl.ds(s, TM_FFN, stride=SUBLANES), :] for s in range(SUBLANES)],
                            axis=1).astype(BF16)
        u = _dot(x, w1_ref[0]) + b1_ref[0]
        f = u.shape[-1] // 2
        glu = jnp.minimum(u[:, :f], SWIGLU_LIMIT)
        lin = jnp.clip(u[:, f:], -SWIGLU_LIMIT, SWIGLU_LIMIT)
        act = glu * _sigmoid(SWIGLU_ALPHA * glu) * (lin + 1.0)
        y = _dot(act.astype(BF16), w2_ref[0]) + b2_ref[0]
        for s in range(SUBLANES):
            ys_ref[pl.ds(s, TM_FFN, stride=SUBLANES), :] = y[:, s * LANES:(s + 1) * LANES]

    @pl.when(t >= nu_ref[0])
    def _():
        ys_ref[...] = jnp.zeros_like(ys_ref)


def _moe_ffn(tile_expert, n_used, hs3, w1, b1, w2, b2):
    n_rows = hs3.shape[0]
    n_tiles = n_rows // TM_FFN
    d = w1.shape[1]
    f2 = w1.shape[2]
    row_map = lambda t, te, nu: (jnp.minimum(t, nu[0] - 1), 0)
    exp_map = lambda t, te, nu: (te[t], 0, 0)
    grid_spec = pltpu.PrefetchScalarGridSpec(
        num_scalar_prefetch=2,
        grid=(n_tiles,),
        in_specs=[pl.BlockSpec((TM_FFN * SUBLANES, LANES), row_map),
                  pl.BlockSpec((1, d, f2), exp_map), pl.BlockSpec((1, 1, f2), exp_map),
                  pl.BlockSpec((1, f2 // 2, d), exp_map), pl.BlockSpec((1, 1, d), exp_map)],
        out_specs=pl.BlockSpec((TM_FFN * SUBLANES, LANES), lambda t, te, nu: (t, 0)),
    )
    ys2 = pl.pallas_call(
        _moe_ffn_kernel,
        out_shape=jax.ShapeDtypeStruct((n_rows * SUBLANES, LANES), F32),
        grid_spec=grid_spec,
        compiler_params=_params("arbitrary"),
        name="moe_ffn",
    )(tile_expert, n_used, hs3.reshape(n_rows * SUBLANES, LANES), w1, b1, w2, b2)
    return ys2.reshape(hs3.shape)


def _combine_kernel(pos_ref, p_ref, ys_hbm, x_ref, mod_ref, x_o, ybuf, abuf, sem):
    def issue(jj, carry):
        for kk in range(TOP_K):
            pltpu.make_async_copy(ys_hbm.at[pos_ref[jj * TOP_K + kk]], ybuf.at[kk * TM + jj], sem).start()
        return carry

    lax.fori_loop(0, TM, issue, 0, unroll=4)
    n = TM * TOP_K
    pltpu.make_async_copy(ys_hbm.at[pl.ds(0, n)], ybuf, sem).wait()

    def mix(jj, carry):
        acc = p_ref[jj * TOP_K] * ybuf[jj]
        for kk in range(1, TOP_K):
            acc = acc + p_ref[jj * TOP_K + kk] * ybuf[kk * TM + jj]
        abuf[pl.ds(pl.multiple_of(jj * SUBLANES, SUBLANES), SUBLANES), :] = acc
        return carry

    lax.fori_loop(0, TM, mix, 0, unroll=8)
    mixed = jnp.concatenate([abuf[pl.ds(s, TM, stride=SUBLANES), :] for s in range(SUBLANES)], axis=1)
    x_o[0] = x_ref[0] + mod_ref[0, 0][5:6] * mixed


def _combine(pos_flat, p_flat, ys3, x, mod):
    bsz, tu, d = x.shape
    nt = tu // TM
    smem_spec = pl.BlockSpec((TM * TOP_K,), lambda b, i: (b * nt + i,), memory_space=pltpu.SMEM)
    return pl.pallas_call(
        _combine_kernel,
        out_shape=jax.ShapeDtypeStruct(x.shape, F32),
        grid=(bsz, nt),
        in_specs=[smem_spec, smem_spec, pl.BlockSpec(memory_space=pl.ANY), _row_spec(d), _mod_spec(d)],
        out_specs=_row_spec(d),
        scratch_shapes=[pltpu.VMEM((TM * TOP_K, SUBLANES, LANES), F32), pltpu.VMEM((TM * SUBLANES, LANES), F32),
                        pltpu.SemaphoreType.DMA(())],
        compiler_params=_params("arbitrary", "arbitrary"),
        name="moe_combine",
    )(pos_flat, p_flat, ys3, x, mod)


def _moe(x, hf, idx_pad, p_pad, mod, w1, b1, w2, b2):
    bsz, tu, d = x.shape
    n_tok = bsz * tu
    n_rows = n_tok * TOP_K + N_EXPERTS * TM_FFN
    n_tiles = n_rows // TM_FFN
    idx = idx_pad.reshape(n_tok, LANES)[:, :TOP_K]
    onehot = jnp.sum((idx[:, :, None] == jnp.arange(N_EXPERTS, dtype=jnp.int32)).astype(jnp.int32), axis=1)
    csum = jnp.cumsum(onehot, axis=0)
    counts = csum[-1]
    padded = ((counts + TM_FFN - 1) // TM_FFN) * TM_FFN
    ends = jnp.cumsum(padded)
    pos = (ends - padded)[idx] + jnp.take_along_axis(csum - onehot, idx, axis=1)
    n_used = (ends[-1] // TM_FFN).astype(jnp.int32)
    tile_start = jnp.arange(n_tiles, dtype=jnp.int32) * TM_FFN
    tile_expert = jnp.sum((tile_start[:, None] >= ends[None, :]).astype(jnp.int32), axis=1)
    last = jnp.take(tile_expert, jnp.maximum(n_used - 1, 0))
    tile_expert = jnp.where(jnp.arange(n_tiles) < n_used, tile_expert, last).astype(jnp.int32)
    pos_flat = pos.reshape(-1).astype(jnp.int32)

    hs3 = _dispatch(pos_flat, hf.reshape(n_tok, d), n_rows)
    ys3 = _moe_ffn(tile_expert, n_used.reshape(1), hs3, w1, b1, w2, b2)
    p_flat = p_pad.reshape(n_tok, LANES)[:, :TOP_K].reshape(-1)
    return _combine(pos_flat, p_flat, ys3, x, mod)


def _rope_tables(n_ctx, n_lat):
    def angles(rot_dim):
        rows = n_lat // GRID_W
        row = jnp.repeat(jnp.arange(rows, dtype=F32), GRID_W)
        col = jnp.tile(jnp.arange(GRID_W, dtype=F32), rows)
        half = rot_dim // 2
        inv_freq = ROPE_THETA ** (-jnp.arange(0, half, 2, dtype=F32) / half)
        return row[:, None] * inv_freq[None, :], col[:, None] * inv_freq[None, :]

    def with_ctx(t, fill):
        return jnp.concatenate([jnp.full((n_ctx, LANES), fill, F32), t], axis=0)

    ar, ac = angles(MLA_ROPE)
    one = jnp.ones((n_lat, 32), F32)
    zero = jnp.zeros((n_lat, 32), F32)
    cm = jnp.concatenate([jnp.cos(ar), jnp.cos(ac), one, jnp.cos(ar), jnp.cos(ac), one], axis=1)
    sm = jnp.concatenate([-jnp.sin(ar), -jnp.sin(ac), zero, jnp.sin(ar), jnp.sin(ac), zero], axis=1)
    ar, ac = angles(GQA_HEAD_DIM)
    cg = jnp.concatenate([jnp.cos(ar), jnp.cos(ac), jnp.cos(ar), jnp.cos(ac)], axis=1)
    sg = jnp.concatenate([-jnp.sin(ar), -jnp.sin(ac), jnp.sin(ar), jnp.sin(ac)], axis=1)
    return with_ctx(cm, 1.0), with_ctx(sm, 0.0), with_ctx(cg, 1.0), with_ctx(sg, 0.0)


_PERM_GQA = np.concatenate([np.arange(0, 32), np.arange(64, 96), np.arange(32, 64), np.arange(96, 128)])
_SRC_MLA = np.zeros(LANES, np.int32)
_MASK_MLA = np.zeros(LANES, np.float32)
_SRC_MLA[0:16], _SRC_MLA[16:32], _SRC_MLA[64:80], _SRC_MLA[80:96] = (
    np.arange(0, 16), np.arange(32, 48), np.arange(16, 32), np.arange(48, 64))
_MASK_MLA[0:32] = 1.0
_MASK_MLA[64:96] = 1.0


def _pad_rope64(w):
    return jnp.take(w, _SRC_MLA, axis=-1) * _MASK_MLA


def _attn_weights(e, w_in, q_norm, w_uq, kv_norm, w_ukv, qn_g, qr_g, kn_g, kr_g, gq_g, gk_g, w_out):
    s1 = MLA_Q_RANK
    s2 = s1 + MLA_KV_RANK
    s3 = s2 + MLA_ROPE
    s4 = s3 + GQA_HEADS * GQA_HEAD_DIM
    s5 = s4 + GQA_KV_HEADS * GQA_HEAD_DIM
    wi = w_in[e]
    d = wi.shape[0]
    gq = wi[:, s3:s4].reshape(d, GQA_HEADS, GQA_HEAD_DIM)[:, :, _PERM_GQA].reshape(d, -1)
    gk = wi[:, s4:s5].reshape(d, GQA_KV_HEADS, GQA_HEAD_DIM)[:, :, _PERM_GQA].reshape(d, -1)
    w_in_p = jnp.concatenate([wi[:, :s2], _pad_rope64(wi[:, s2:s3]), gq, gk, wi[:, s5:]], axis=1)
    uq = w_uq[e].reshape(MLA_Q_RANK, MLA_HEADS, MLA_NOPE + MLA_ROPE)
    uq_p = jnp.concatenate([uq[:, :, :MLA_NOPE], _pad_rope64(uq[:, :, MLA_NOPE:])], axis=-1)
    row = lambda g: g.reshape(1, -1).astype(F32)
    return dict(
        w_in=w_in_p.astype(BF16), q_norm=row(q_norm[e]), w_uq=uq_p.reshape(MLA_Q_RANK, -1).astype(BF16),
        kv_norm=row(kv_norm[e]), w_ukv=w_ukv[e].astype(BF16),
        qn_g=row(qn_g[e]), qr_g=row(_pad_rope64(qr_g[e])), kn_g=row(kn_g[e]), kr_g=row(_pad_rope64(kr_g[e])),
        gq_g=row(gq_g[e][_PERM_GQA]), gk_g=row(gk_g[e][_PERM_GQA]), w_out=w_out[e].astype(BF16))


def _pad_cols(w, n):
    return jnp.pad(w, ((0, 0), (0, n - w.shape[1])))


def _pad_rows(w, n):
    return jnp.pad(w, ((0, n - w.shape[0]), (0, 0)))


def _lora_pair(w1, w2):
    rank = w1.shape[-1]
    down = jnp.concatenate([w1[0], w1[1]], axis=1)
    up = jnp.stack([jnp.pad(w2[0], ((0, rank), (0, 0))), jnp.pad(w2[1], ((rank, 0), (0, 0)))])
    return down.astype(BF16), up.astype(BF16)


def _rwkv_weights(j, mix, w_r, w_k, w_v, w_o, w0, w1, w2, a0, a1, a2, v0, v1, v2, g1, g2, k_k, k_a, r_k, ln_w, ln_b):
    d = w_r.shape[1]
    row = lambda g: g.reshape(1, -1).astype(F32)
    heads = np.arange(d) // RWKV_HEAD
    hs1 = (heads[:, None] == np.arange(LANES)[None, :]).astype(np.float32)
    w1c, w2p = _lora_pair(w1[j], w2[j])
    a1c, a2p = _lora_pair(a1[j], a2[j])
    rp = dict(mix=mix[j], w_r=w_r[j].astype(BF16), w_k=w_k[j].astype(BF16), w_v=w_v[j].astype(BF16),
              w_o=w_o[j].astype(BF16), w0=w0[j], w1=w1c, w2=w2p, a0=a0[j], a1=a1c, a2=a2p,
              g1=g1[j].astype(BF16), g2=g2[j].astype(BF16), k_k=row(k_k[j]), k_a=row(k_a[j]), r_k=row(r_k[j]),
              ln_w=row(ln_w[j]), ln_b=row(ln_b[j]),
              hs1=jnp.asarray(hs1, BF16), hs2=jnp.asarray(hs1.T, BF16))
    if j > 0:
        rp.update(v0=row(v0[j - 1]), v1=_pad_cols(v1[j - 1], LANES).astype(BF16),
                  v2=_pad_rows(v2[j - 1], LANES).astype(BF16))
    return rp


def kernel(x, c, ctx, c_ctx, ada_w, ada_b, norm_mix, norm_ffn, attn_w_in, mla_q_norm, mla_w_uq, mla_kv_norm, mla_w_ukv, mla_qn_g, mla_qr_g, mla_kn_g, mla_kr_g, gqa_q_g, gqa_k_g, attn_w_out, rwkv_mix, rwkv_w_r, rwkv_w_k, rwkv_w_v, rwkv_w_o, rwkv_w0, rwkv_w1, rwkv_w2, rwkv_a0, rwkv_a1, rwkv_a2, rwkv_v0, rwkv_v1, rwkv_v2, rwkv_g1, rwkv_g2, rwkv_k_k, rwkv_k_a, rwkv_r_k, rwkv_ln_w, rwkv_ln_b, moe_router_w, moe_router_b, moe_w1, moe_b1, moe_w2, moe_b2):
    bsz, n_lat, d = x.shape
    n_ctx = ctx.shape[1]
    depth = ada_w.shape[0]
    tu = n_ctx + n_lat
    assert n_ctx % TM == 0 and n_lat % TM == 0 and n_ctx % WKV_CHUNK == 0 and d == SUBLANES * LANES
    assert (bsz * tu * TOP_K) % TM_FFN == 0

    rows = -(-(bsz + 1) // SUBLANES) * SUBLANES
    cond = jnp.zeros((rows, d), F32).at[:bsz].set(c).at[bsz].set(c_ctx)
    mods = _adaln(cond, ada_w, ada_b)
    mods_lat = mods[:, :bsz].reshape(depth, bsz, 1, N_MOD, d)
    mods_ctx = jnp.broadcast_to(mods[:, bsz].reshape(depth, 1, 1, N_MOD, d), mods_lat.shape)
    mods = jnp.concatenate([mods_ctx, mods_lat], axis=2)

    tabs = _rope_tables(n_ctx, n_lat)
    xs = jnp.concatenate([ctx, x], axis=1)

    n_e, _, f2 = moe_w1.shape[1:]
    rw_all = jnp.pad(moe_router_w, ((0, 0), (0, 0), (0, LANES - n_e)))
    rb_all = jnp.pad(moe_router_b, ((0, 0), (0, LANES - n_e)), constant_values=-1e30).reshape(depth, 1, LANES)

    w1_all = _deinterleave_w1(moe_w1)

    v_first = None
    for i in range(depth):
        mod = mods[i]
        nm = norm_mix[i].reshape(1, d)
        nf = norm_ffn[i].reshape(1, d)
        rw, rb = rw_all[i], rb_all[i]
        if i % 2 == 0:
            aw = _attn_weights(i // 2, attn_w_in, mla_q_norm, mla_w_uq, mla_kv_norm, mla_w_ukv, mla_qn_g,
                               mla_qr_g, mla_kn_g, mla_kr_g, gqa_q_g, gqa_k_g, attn_w_out)
            qm, km, vm, qg, kg, vg = _attn_proj(xs, mod, nm, aw, tabs)
            om = _attention(qm, km, vm, n_ctx)
            og = _attention(qg, kg, vg, n_ctx)
            xs, hf, idx_pad, p_pad = _attn_out(om, og, xs, mod, aw["w_out"], nf, rw, rb)
        else:
            j = i // 2
            rp = _rwkv_weights(j, rwkv_mix, rwkv_w_r, rwkv_w_k, rwkv_w_v, rwkv_w_o, rwkv_w0, rwkv_w1, rwkv_w2,
                               rwkv_a0, rwkv_a1, rwkv_a2, rwkv_v0, rwkv_v1, rwkv_v2, rwkv_g1, rwkv_g2,
                               rwkv_k_k, rwkv_k_a, rwkv_r_k, rwkv_ln_w, rwkv_ln_b)
            r, v, kk, w, k, b, g, bv = _rwkv_feat(xs, mod, nm, rp, v_first if j > 0 else None)
            if j == 0:
                v_first = v
            y = _wkv_scan(r, v, kk, w, k, b, n_ctx)
            xs, hf, idx_pad, p_pad = _rwkv_out(y, g, bv, xs, mod, rp, nf, rw, rb)
        w1 = w1_all[i]
        b1 = jnp.concatenate([moe_b1[i][:, 0::2], moe_b1[i][:, 1::2]], axis=-1).reshape(n_e, 1, f2)
        w2 = moe_w2[i].astype(BF16)
        b2 = moe_b2[i].reshape(n_e, 1, d)
        xs = _moe(xs, hf, idx_pad, p_pad, mod, w1, b1, w2, b2)
    return xs[:, n_ctx:]
```
